```python
import jax, jax.numpy as jnp
from jax import lax
import numpy as np

D_MODEL = 2048
BATCH = 2
SEQ = 16384
DEPTH = 2

CHUNK = 64
N_GROUPS = 4
W_G = D_MODEL // N_GROUPS
D_MIX = N_GROUPS * W_G
GMLP_BLOCK = 128
GMLP_HEADS = 4
GMLP_HEAD_DIM = W_G // GMLP_HEADS
MLA_NOPE = 128
MLA_ROPE = 64
MLA_V = 128
MLA_HEADS = W_G // MLA_V
Q_LORA = 3 * D_MODEL // 16
KV_LORA = D_MODEL // 8
ROPE_THETA = 10000.0
Q_BLOCK = 128
SCONV_K = 3
CONF_K = 31
D_FF = 11 * D_MODEL // 4
FFN_K = 3
LN_EPS = 1e-5
RMS_EPS = 1e-6
ALPHA = (2.0 * DEPTH) ** 0.25
BETA = (8.0 * DEPTH) ** -0.25
IN_WIDTHS = (2 * W_G, Q_LORA, KV_LORA, MLA_ROPE, W_G, W_G, W_G, 2 * W_G)
IN_COLS = sum(IN_WIDTHS)
IN_SPLITS = tuple(sum(IN_WIDTHS[:i + 1]) for i in range(len(IN_WIDTHS) - 1))

kernel_name = "hybrid_headgroup_streaming_encoder"


def layer_norm(x, gain=None, bias=None, eps=LN_EPS):
    xf = x.astype(jnp.float32)
    mu = jnp.mean(xf, axis=-1, keepdims=True)
    var = jnp.mean(jnp.square(xf - mu), axis=-1, keepdims=True)
    y = (xf - mu) * lax.rsqrt(var + eps)
    if gain is not None:
        y = y * gain.astype(jnp.float32) + bias.astype(jnp.float32)
    return y.astype(x.dtype)


def rms_norm(x, gain, eps=RMS_EPS):
    xf = x.astype(jnp.float32)
    y = xf * lax.rsqrt(jnp.mean(jnp.square(xf), axis=-1, keepdims=True) + eps)
    return (y * gain.astype(jnp.float32)).astype(x.dtype)


def causal_dwconv(x, w, b=None):
    k = w.shape[0]
    y = lax.conv_general_dilated(
        x, w[:, None, :].astype(x.dtype), window_strides=(1,), padding=[(k - 1, 0)],
        dimension_numbers=("NWC", "WIO", "NWC"), feature_group_count=x.shape[-1])
    if b is not None:
        y = y + b.astype(x.dtype)
    return y


def apply_rope(x, cos, sin):
    x1, x2 = jnp.split(x, 2, axis=-1)
    return jnp.concatenate([x1 * cos - x2 * sin, x2 * cos + x1 * sin], axis=-1)


def gmlp_mixer(uv, ln_g, ln_b, w_s, b_s):
    z = jax.nn.gelu(uv)
    u, v = jnp.split(z, 2, axis=-1)
    v = layer_norm(v, ln_g, ln_b)
    bn, s, _ = v.shape
    vb = v.reshape(bn, s // GMLP_BLOCK, GMLP_BLOCK, GMLP_HEADS, GMLP_HEAD_DIM)
    cpos = jnp.arange(GMLP_BLOCK) // CHUNK
    mask = cpos[None, :] <= cpos[:, None]
    w = jnp.where(mask[None], w_s, 0.0).astype(v.dtype)
    mixed = jnp.einsum("hij,bnjhd->bnihd", w, vb) + b_s.T.astype(v.dtype)[:, :, None]
    return u * mixed.reshape(bn, s, W_G)


def mla_mixer(q_c, kv_c, k_r, q_norm_g, w_uq, kv_norm_g, w_ukv, cos, sin):
    bn, s, _ = q_c.shape
    q = (rms_norm(q_c, q_norm_g) @ w_uq).reshape(bn, s, MLA_HEADS, MLA_NOPE + MLA_ROPE)
    q_nope, q_rope = q[..., :MLA_NOPE], q[..., MLA_NOPE:]
    q_rope = apply_rope(q_rope, cos[:, None, :], sin[:, None, :])
    kv = (rms_norm(kv_c, kv_norm_g) @ w_ukv).reshape(bn, s, MLA_HEADS, MLA_NOPE + MLA_V)
    k_nope, v = kv[..., :MLA_NOPE], kv[..., MLA_NOPE:]
    k_rope = apply_rope(k_r, cos, sin)
    scale = (MLA_NOPE + MLA_ROPE) ** -0.5
    nqb = s // Q_BLOCK
    qn_b = q_nope.reshape(bn, nqb, Q_BLOCK, MLA_HEADS, MLA_NOPE).transpose(1, 0, 2, 3, 4)
    qr_b = q_rope.reshape(bn, nqb, Q_BLOCK, MLA_HEADS, MLA_ROPE).transpose(1, 0, 2, 3, 4)
    k_chunk = jnp.arange(s) // CHUNK
    qc_b = k_chunk.reshape(nqb, Q_BLOCK)

    def block(args):
        qn, qr, qc = args
        sc = (jnp.einsum("bqhd,bkhd->bhqk", qn, k_nope)
              + jnp.einsum("bqhr,bkr->bhqk", qr, k_rope)).astype(jnp.float32) * scale
        mask = k_chunk[None, :] <= qc[:, None]
        sc = jnp.where(mask[None, None], sc, -jnp.inf)
        p = jax.nn.softmax(sc, axis=-1).astype(v.dtype)
        return jnp.einsum("bhqk,bkhd->bqhd", p, v)

    o = lax.map(block, (qn_b, qr_b, qc_b))
    return o.transpose(1, 0, 2, 3, 4).reshape(bn, s, MLA_HEADS * MLA_V)


def short_conv_mixer(b_gate, c_gate, h, w_conv):
    return b_gate * causal_dwconv(c_gate * h, w_conv)


def conformer_conv(ag, w_dw, b_dw, ln_g, ln_b):
    a, g = jnp.split(ag, 2, axis=-1)
    y = a * jax.nn.sigmoid(g)
    y = causal_dwconv(y, w_dw, b_dw)
    return jax.nn.silu(layer_norm(y, ln_g, ln_b))


def conv_ffn(h, w_up, w_conv, b_conv, w_down):
    up = h @ w_up
    gate, val = jnp.split(up, 2, axis=-1)
    gate = causal_dwconv(gate, w_conv, b_conv)
    return (jax.nn.silu(gate) * val) @ w_down


def setup_inputs(seed: int = 0) -> dict:
    key = jax.random.key(seed)
    ks = jax.random.split(key, 32)
    L = DEPTH

    def nrm(k, shape, scale):
        return jax.random.normal(k, shape, jnp.float32) * scale

    return {
        "x": nrm(ks[0], (BATCH, SEQ, D_MODEL), 1.0),
        "c": nrm(ks[1], (BATCH, D_MODEL), 1.0),
        "w_mod": nrm(ks[2], (L, D_MODEL, 6 * D_MODEL), 0.5 * D_MODEL ** -0.5),
        "b_mod": nrm(ks[3], (L, 6 * D_MODEL), 0.01),
        "w_in": nrm(ks[4], (L, D_MODEL, IN_COLS), D_MODEL ** -0.5),
        "gmlp_ln_g": 1.0 + nrm(ks[5], (L, W_G), 0.02),
        "gmlp_ln_b": nrm(ks[6], (L, W_G), 0.02),
        "gmlp_w_s": nrm(ks[7], (L, GMLP_HEADS, GMLP_BLOCK, GMLP_BLOCK), GMLP_BLOCK ** -0.5),
        "gmlp_b_s": nrm(ks[8], (L, GMLP_HEADS, GMLP_BLOCK), 0.02),
        "mla_q_norm": 1.0 + nrm(ks[9], (L, Q_LORA), 0.02),
        "mla_w_uq": nrm(ks[10], (L, Q_LORA, MLA_HEADS * (MLA_NOPE + MLA_ROPE)), Q_LORA ** -0.5),
        "mla_kv_norm": 1.0 + nrm(ks[11], (L, KV_LORA), 0.02),
        "mla_w_ukv": nrm(ks[12], (L, KV_LORA, MLA_HEADS * (MLA_NOPE + MLA_V)), KV_LORA ** -0.5),
        "sconv_w": nrm(ks[13], (L, SCONV_K, W_G), SCONV_K ** -0.5),
        "conf_w_dw": nrm(ks[14], (L, CONF_K, W_G), CONF_K ** -0.5),
        "conf_b_dw": nrm(ks[15], (L, W_G), 0.02),
        "conf_ln_g": 1.0 + nrm(ks[16], (L, W_G), 0.02),
        "conf_ln_b": nrm(ks[17], (L, W_G), 0.02),
        "w_out": nrm(ks[18], (L, D_MIX, D_MODEL), BETA * D_MIX ** -0.5),
        "post_mix_g": 1.0 + nrm(ks[19], (L, D_MODEL), 0.02),
        "post_mix_b": nrm(ks[20], (L, D_MODEL), 0.02),
        "ffn_w_up": nrm(ks[21], (L, D_MODEL, 2 * D_FF), D_MODEL ** -0.5),
        "ffn_w_conv": nrm(ks[22], (L, FFN_K, D_FF), FFN_K ** -0.5),
        "ffn_b_conv": nrm(ks[23], (L, D_FF), 0.02),
        "ffn_w_down": nrm(ks[24], (L, D_FF, D_MODEL), BETA * D_FF ** -0.5),
        "post_ffn_g": 1.0 + nrm(ks[25], (L, D_MODEL), 0.02),
        "post_ffn_b": nrm(ks[26], (L, D_MODEL), 0.02),
    }


def reference(x, c, w_mod, b_mod, w_in, gmlp_ln_g, gmlp_ln_b, gmlp_w_s, gmlp_b_s,
              mla_q_norm, mla_w_uq, mla_kv_norm, mla_w_ukv, sconv_w, conf_w_dw, conf_b_dw,
              conf_ln_g, conf_ln_b, w_out, post_mix_g, post_mix_b, ffn_w_up, ffn_w_conv,
              ffn_b_conv, ffn_w_down, post_ffn_g, post_ffn_b):
    bn, s, d = x.shape
    pos = jnp.arange(s, dtype=jnp.float32)
    inv_freq = ROPE_THETA ** (-jnp.arange(0, MLA_ROPE, 2, dtype=jnp.float32) / MLA_ROPE)
    ang = pos[:, None] * inv_freq[None, :]
    cos = jnp.cos(ang).astype(x.dtype)
    sin = jnp.sin(ang).astype(x.dtype)
    c_act = jax.nn.silu(c)

    for l in range(DEPTH):
        mod = (c_act @ w_mod[l] + b_mod[l]).reshape(bn, 6, d)
        shift_m, scale_m, gate_m = mod[:, 0, None], mod[:, 1, None], mod[:, 2, None]
        shift_f, scale_f, gate_f = mod[:, 3, None], mod[:, 4, None], mod[:, 5, None]

        h = layer_norm(x) * (1.0 + scale_m) + shift_m
        proj = h @ w_in[l]
        uv, q_c, kv_c, k_r, sb, sc, sh, ag = jnp.split(proj, IN_SPLITS, axis=-1)
        y_a = gmlp_mixer(uv, gmlp_ln_g[l], gmlp_ln_b[l], gmlp_w_s[l], gmlp_b_s[l])
        y_b = mla_mixer(q_c, kv_c, k_r, mla_q_norm[l], mla_w_uq[l], mla_kv_norm[l],
                        mla_w_ukv[l], cos, sin)
        y_c = short_conv_mixer(sb, sc, sh, sconv_w[l])
        y_d = conformer_conv(ag, conf_w_dw[l], conf_b_dw[l], conf_ln_g[l], conf_ln_b[l])
        y = jnp.concatenate([y_a, y_b, y_c, y_d], axis=-1) @ w_out[l]
        x = layer_norm(ALPHA * x + (1.0 + gate_m) * y, post_mix_g[l], post_mix_b[l])

        h = layer_norm(x) * (1.0 + scale_f) + shift_f
        y = conv_ffn(h, ffn_w_up[l], ffn_w_conv[l], ffn_b_conv[l], ffn_w_down[l])
        x = layer_norm(ALPHA * x + (1.0 + gate_f) * y, post_ffn_g[l], post_ffn_b[l])
    return x
```

```python
import functools

import jax
import jax.numpy as jnp
from jax import lax
from jax.experimental import pallas as pl
from jax.experimental.pallas import tpu as pltpu

F32 = jnp.float32
BF16 = jnp.bfloat16

CHUNK = 64
N_GROUPS = 4
GMLP_BLOCK = 128
GMLP_HEADS = 4
MLA_NOPE = 128
MLA_ROPE = 64
MLA_V = 128
ROPE_THETA = 10000.0
SCONV_K = 3
CONF_K = 31
FFN_K = 3
LN_EPS = 1e-5
RMS_EPS = 1e-6

LANES = 128
SUBLANES = 8
VMEM_LIMIT_BYTES = 56 * 1024 * 1024

NEG_BIG = -1e30


def _tiles(seq):
    def pick(pref):
        t = min(pref, seq)
        assert seq % t == 0
        return t
    return dict(inproj=pick(512), attn=pick(512), outproj=pick(512),
                ffn_up=pick(1024), ffn_up_cols=512, ffn_down=pick(256), mod_cols=1024)


def _resident(shape):
    nd = len(shape)
    return pl.BlockSpec(shape, lambda *_: (0,) * nd, pipeline_mode=pl.Buffered(1))


def _params(n_grid):
    return pltpu.CompilerParams(dimension_semantics=("arbitrary",) * n_grid,
                                vmem_limit_bytes=VMEM_LIMIT_BYTES)


def _norm_rows(x):
    mu = jnp.mean(x, axis=-1, keepdims=True)
    xc = x - mu
    var = jnp.mean(xc * xc, axis=-1, keepdims=True)
    return xc * lax.rsqrt(var + LN_EPS)


def _rms_rows(x, gain):
    return x * lax.rsqrt(jnp.mean(x * x, axis=-1, keepdims=True) + RMS_EPS) * gain


def _dot(a, b):
    return jnp.dot(a, b, preferred_element_type=F32)


def _mod_kernel(c_ref, w_ref, b_ref, o_ref):
    c = c_ref[...]
    c_act = c * jax.nn.sigmoid(c)
    o_ref[0] = jnp.dot(c_act, w_ref[0], preferred_element_type=F32,
                       precision=lax.Precision.HIGHEST) + b_ref[0]


def _modulation(c, w_mod, b_mod, tn):
    depth, d, n = w_mod.shape
    bn = c.shape[0]
    rows = -(-bn // SUBLANES) * SUBLANES
    c_pad = jnp.zeros((rows, d), F32).at[:bn].set(c)
    out = pl.pallas_call(
        _mod_kernel,
        grid=(depth, n // tn),
        in_specs=[pl.BlockSpec((rows, d), lambda l, j: (0, 0)),
                  pl.BlockSpec((1, d, tn), lambda l, j: (l, 0, j)),
                  pl.BlockSpec((1, 1, tn), lambda l, j: (l, 0, j))],
        out_specs=pl.BlockSpec((1, rows, tn), lambda l, j: (l, 0, j)),
        out_shape=jax.ShapeDtypeStruct((depth, rows, n), F32),
        compiler_params=_params(2),
        name="modulation",
    )(c_pad, w_mod, b_mod.reshape(depth, 1, n))
    return out[:, :bn].reshape(depth, bn, 6, d)


def _causal_taps(buf_ref, w_ref, n_taps, halo, tm):
    acc = None
    for k in range(n_taps):
        term = buf_ref[pl.ds(halo - (n_taps - 1) + k, tm), :] * w_ref[k:k + 1, :]
        acc = term if acc is None else acc + term
    return acc


def _push_halo(buf_ref, new, halo, tm, first):
    @pl.when(first)
    def _():
        buf_ref[pl.ds(0, halo), :] = jnp.zeros((halo, buf_ref.shape[1]), F32)

    @pl.when(jnp.logical_not(first))
    def _():
        buf_ref[pl.ds(0, halo), :] = buf_ref[pl.ds(tm, halo), :]

    buf_ref[pl.ds(halo, tm), :] = new


def _inproj_kernel(x_ref, mod_ref, w_ref, cs_ref, sn_ref,
                   gln_g_ref, gln_b_ref, ws_ref, bs_ref,
                   qn_g_ref, wuq_ref, kvn_g_ref, wukv_ref,
                   sconv_w_ref, cw_ref, cb_ref, cln_g_ref, cln_b_ref,
                   yacd_ref, q_ref, k_ref, vt_ref,
                   sbuf_ref, cbuf_ref, *, tm, wg, q_lora, kv_lora, heads, scale,
                   s_halo, c_halo):
    first = pl.program_id(1) == 0
    mod = mod_ref[0]
    h = (_norm_rows(x_ref[0]) * (1.0 + mod[1:2, :]) + mod[0:1, :]).astype(BF16)

    o_uv = 0
    o_q = o_uv + 2 * wg
    o_kv = o_q + q_lora
    o_kr = o_kv + kv_lora
    o_s = o_kr + 2 * LANES
    o_ag = o_s + 3 * wg

    z = jax.nn.gelu(_dot(h, w_ref[:, o_uv:o_uv + 2 * wg]), approximate=True)
    u = z[:, :wg]
    v = (_norm_rows(z[:, wg:]) * gln_g_ref[...] + gln_b_ref[...]).astype(BF16)
    nblk = tm // GMLP_BLOCK
    hd = wg // GMLP_HEADS
    ii = lax.broadcasted_iota(jnp.int32, (GMLP_BLOCK, GMLP_BLOCK), 0) // CHUNK
    jj = lax.broadcasted_iota(jnp.int32, (GMLP_BLOCK, GMLP_BLOCK), 1) // CHUNK
    bias = bs_ref[...]
    mixed_cols = []
    for hh in range(GMLP_HEADS):
        w_h = jnp.where(jj <= ii, ws_ref[hh], 0.0).astype(BF16)
        v_h = jnp.concatenate(
            [v[n * GMLP_BLOCK:(n + 1) * GMLP_BLOCK, hh * hd:(hh + 1) * hd] for n in range(nblk)],
            axis=1)
        mixed_cols.append(_dot(w_h, v_h))
    for n in range(nblk):
        mixed = jnp.concatenate([m[:, n * hd:(n + 1) * hd] for m in mixed_cols], axis=1) + bias
        rows = slice(n * GMLP_BLOCK, (n + 1) * GMLP_BLOCK)
        yacd_ref[0, rows, 0:wg] = (u[rows, :] * mixed).astype(BF16)

    cs = cs_ref[...]
    sn = sn_ref[...]
    q_lat = _rms_rows(_dot(h, w_ref[:, o_q:o_q + q_lora]), qn_g_ref[...]).astype(BF16)
    qq = _dot(q_lat, wuq_ref[...])
    kv_lat = _rms_rows(_dot(h, w_ref[:, o_kv:o_kv + kv_lora]), kvn_g_ref[...]).astype(BF16)
    kv = _dot(kv_lat, wukv_ref[...])
    kr2 = _dot(h, w_ref[:, o_kr:o_kr + 2 * LANES])
    k_rope = (kr2[:, :LANES] * cs + kr2[:, LANES:] * sn).astype(BF16)
    hw = 2 * LANES
    for hh in range(heads):
        q_nope = qq[:, hh * hw:hh * hw + LANES]
        q_rope = qq[:, hh * hw + LANES:(hh + 1) * hw]
        q_rot = qq[:, heads * hw + hh * LANES:heads * hw + (hh + 1) * LANES]
        q_ref[0, :, hh * hw:hh * hw + LANES] = (q_nope * scale).astype(BF16)
        q_ref[0, :, hh * hw + LANES:(hh + 1) * hw] = ((q_rope * cs + q_rot * sn) * scale).astype(BF16)
        k_ref[0, :, hh * hw:hh * hw + LANES] = kv[:, hh * hw:hh * hw + LANES].astype(BF16)
        k_ref[0, :, hh * hw + LANES:(hh + 1) * hw] = k_rope
        vt_ref[0, 0, hh * LANES:(hh + 1) * LANES, :] = kv[:, hh * hw + LANES:(hh + 1) * hw].T.astype(BF16)

    s3 = _dot(h, w_ref[:, o_s:o_s + 3 * wg])
    _push_halo(sbuf_ref, s3[:, wg:2 * wg] * s3[:, 2 * wg:], s_halo, tm, first)
    y_c = s3[:, :wg] * _causal_taps(sbuf_ref, sconv_w_ref, SCONV_K, s_halo, tm)
    yacd_ref[0, :, wg:2 * wg] = y_c.astype(BF16)

    ag = _dot(h, w_ref[:, o_ag:o_ag + 2 * wg])
    _push_halo(cbuf_ref, ag[:, :wg] * jax.nn.sigmoid(ag[:, wg:]), c_halo, tm, first)
    y = _causal_taps(cbuf_ref, cw_ref, CONF_K, c_halo, tm) + cb_ref[...]
    y = _norm_rows(y) * cln_g_ref[...] + cln_b_ref[...]
    yacd_ref[0, :, 2 * wg:3 * wg] = (y * jax.nn.sigmoid(y)).astype(BF16)


def _pack_inproj_weights(w_in_l, wg, q_lora, kv_lora):
    d = w_in_l.shape[0]
    o = 0
    w_uv = w_in_l[:, o:o + 2 * wg]; o += 2 * wg
    w_q = w_in_l[:, o:o + q_lora]; o += q_lora
    w_kv = w_in_l[:, o:o + kv_lora]; o += kv_lora
    w_kr = w_in_l[:, o:o + MLA_ROPE]; o += MLA_ROPE
    w_s = w_in_l[:, o:o + 3 * wg]; o += 3 * wg
    w_ag = w_in_l[:, o:o + 2 * wg]
    half = MLA_ROPE // 2
    pad = jnp.zeros((d, LANES - MLA_ROPE), w_in_l.dtype)
    w_kr_rot = jnp.concatenate([-w_kr[:, half:], w_kr[:, :half]], axis=1)
    return jnp.concatenate([w_uv, w_q, w_kv, w_kr, pad, w_kr_rot, pad, w_s, w_ag],
                           axis=1).astype(BF16)


def _pack_uq_weights(w_uq_l, heads):
    r = w_uq_l.shape[0]
    half = MLA_ROPE // 2
    w = w_uq_l.reshape(r, heads, MLA_NOPE + MLA_ROPE)
    pad = jnp.zeros((r, heads, LANES - MLA_ROPE), w.dtype)
    rope = w[:, :, MLA_NOPE:]
    main = jnp.concatenate([w[:, :, :MLA_NOPE], rope, pad], axis=2).reshape(r, heads * 2 * LANES)
    rot = jnp.concatenate([-rope[:, :, half:], rope[:, :, :half], pad], axis=2).reshape(r, heads * LANES)
    return jnp.concatenate([main, rot], axis=1).astype(BF16)


def _rope_tables(seq):
    pos = jnp.arange(seq, dtype=F32)
    inv_freq = ROPE_THETA ** (-jnp.arange(0, MLA_ROPE, 2, dtype=F32) / MLA_ROPE)
    ang = pos[:, None] * inv_freq[None, :]
    zeros = jnp.zeros((seq, LANES - MLA_ROPE), F32)
    cos, sin = jnp.cos(ang), jnp.sin(ang)
    return (jnp.concatenate([cos, cos, zeros], axis=1), jnp.concatenate([sin, sin, zeros], axis=1))


def _inproj(x, mod_l, w_packed, cs, sn, gln_g, gln_b, w_s, b_s_tile, qn_g, wuq, kvn_g, wukv,
            sconv_w, conf_w, conf_b, cln_g, cln_b, *, tm, wg, heads):
    bn, seq, d = x.shape
    q_lora, kv_lora = wuq.shape[0], wukv.shape[0]
    s_halo = SUBLANES
    c_halo = -(-(CONF_K - 1) // SUBLANES) * SUBLANES
    kern = functools.partial(
        _inproj_kernel, tm=tm, wg=wg, q_lora=q_lora, kv_lora=kv_lora, heads=heads,
        scale=float((MLA_NOPE + MLA_ROPE) ** -0.5), s_halo=s_halo, c_halo=c_halo)
    row = lambda b, i: (b, i, 0)
    smalls = [gln_g, gln_b, w_s, b_s_tile, qn_g, wuq, kvn_g, wukv, sconv_w, conf_w, conf_b,
              cln_g, cln_b]
    return pl.pallas_call(
        kern,
        grid=(bn, seq // tm),
        in_specs=[pl.BlockSpec((1, tm, d), row),
                  pl.BlockSpec((1, 6, d), lambda b, i: (b, 0, 0)),
                  _resident(w_packed.shape),
                  pl.BlockSpec((tm, LANES), lambda b, i: (i, 0)),
                  pl.BlockSpec((tm, LANES), lambda b, i: (i, 0))]
                 + [_resident(a.shape) for a in smalls],
        out_specs=[pl.BlockSpec((1, tm, 3 * wg), row),
                   pl.BlockSpec((1, tm, heads * 2 * LANES), row),
                   pl.BlockSpec((1, tm, heads * 2 * LANES), row),
                   pl.BlockSpec((1, 1, heads * LANES, tm), lambda b, i: (b, i, 0, 0))],
        out_shape=[jax.ShapeDtypeStruct((bn, seq, 3 * wg), BF16),
                   jax.ShapeDtypeStruct((bn, seq, heads * 2 * LANES), BF16),
                   jax.ShapeDtypeStruct((bn, seq, heads * 2 * LANES), BF16),
                   jax.ShapeDtypeStruct((bn, seq // tm, heads * LANES, tm), BF16)],
        scratch_shapes=[pltpu.VMEM((s_halo + tm, wg), F32),
                        pltpu.VMEM((c_halo + tm, wg), F32)],
        compiler_params=_params(2),
        name="inproj_mixers",
    )(x, mod_l, w_packed, cs, sn, *smalls)


def _attn_kernel(q_ref, k_ref, vt_ref, o_ref, m_ref, l_ref, acc_ref, *, tq, tk):
    qi = pl.program_id(2)
    q = q_ref[0]
    m_ref[...] = jnp.full(m_ref.shape, NEG_BIG, F32)
    l_ref[...] = jnp.zeros(l_ref.shape, F32)
    acc_ref[...] = jnp.zeros(acc_ref.shape, F32)

    def step(j, masked):
        kb = k_ref[0, pl.ds(pl.multiple_of(j * tk, tk), tk), :]
        vtb = vt_ref[0, j]
        st = lax.dot_general(kb, q, (((1,), (1,)), ((), ())), preferred_element_type=F32)
        if masked:
            kc = lax.broadcasted_iota(jnp.int32, (tk, tq), 0) // CHUNK
            qc = lax.broadcasted_iota(jnp.int32, (tk, tq), 1) // CHUNK
            st = jnp.where(kc <= qc, st, NEG_BIG)
        m_prev = m_ref[...]
        m_new = jnp.maximum(m_prev, jnp.max(st, axis=0, keepdims=True))
        alpha = jnp.exp(m_prev - m_new)
        p = jnp.exp(st - m_new)
        l_ref[...] = alpha * l_ref[...] + jnp.sum(p, axis=0, keepdims=True)
        acc_ref[...] = alpha * acc_ref[...] + _dot(vtb, p.astype(BF16))
        m_ref[...] = m_new

    def body(j, carry):
        step(j, masked=False)
        return carry

    lax.fori_loop(0, qi, body, 0)
    step(qi, masked=True)
    o_ref[0] = (acc_ref[...] / l_ref[...]).T.astype(BF16)


def _attention(q, k, vt, *, tq, heads):
    bn, seq, _ = q.shape
    tk = vt.shape[3]
    assert tq == tk
    kern = functools.partial(_attn_kernel, tq=tq, tk=tk)
    return pl.pallas_call(
        kern,
        grid=(bn, heads, seq // tq),
        in_specs=[pl.BlockSpec((1, tq, 2 * LANES), lambda b, h, i: (b, i, h)),
                  pl.BlockSpec((1, seq, 2 * LANES), lambda b, h, i: (b, 0, h)),
                  pl.BlockSpec((1, seq // tk, LANES, tk), lambda b, h, i: (b, 0, h, 0))],
        out_specs=pl.BlockSpec((1, tq, LANES), lambda b, h, i: (b, i, h)),
        out_shape=jax.ShapeDtypeStruct((bn, seq, heads * LANES), BF16),
        scratch_shapes=[pltpu.VMEM((1, tq), F32), pltpu.VMEM((1, tq), F32),
                        pltpu.VMEM((MLA_V, tq), F32)],
        compiler_params=_params(3),
        name="mla_attention",
    )(q, k, vt)


def _outproj_kernel(yacd_ref, yb_ref, x_ref, mod_ref, wacd_ref, wb_ref, g_ref, b_ref,
                    x1_ref, h2_ref, *, alpha):
    mod = mod_ref[0]
    y = _dot(yacd_ref[0], wacd_ref[...]) + _dot(yb_ref[0], wb_ref[...])
    x1 = _norm_rows(alpha * x_ref[0] + (1.0 + mod[2:3, :]) * y) * g_ref[...] + b_ref[...]
    x1_ref[0] = x1
    h2_ref[0] = (_norm_rows(x1) * (1.0 + mod[4:5, :]) + mod[3:4, :]).astype(BF16)


def _outproj(yacd, yb, x, mod_l, w_acd, w_b, g, b, *, tm, alpha):
    bn, seq, d = x.shape
    row = lambda bb, i: (bb, i, 0)
    return pl.pallas_call(
        functools.partial(_outproj_kernel, alpha=alpha),
        grid=(bn, seq // tm),
        in_specs=[pl.BlockSpec((1, tm, yacd.shape[2]), row),
                  pl.BlockSpec((1, tm, yb.shape[2]), row),
                  pl.BlockSpec((1, tm, d), row),
                  pl.BlockSpec((1, 6, d), lambda bb, i: (bb, 0, 0)),
                  _resident(w_acd.shape), _resident(w_b.shape),
                  _resident(g.shape), _resident(b.shape)],
        out_specs=[pl.BlockSpec((1, tm, d), row), pl.BlockSpec((1, tm, d), row)],
        out_shape=[jax.ShapeDtypeStruct((bn, seq, d), F32),
                   jax.ShapeDtypeStruct((bn, seq, d), BF16)],
        compiler_params=_params(2),
        name="outproj_postln",
    )(yacd, yb, x, mod_l, w_acd, w_b, g, b)


def _ffn_up_kernel(h_ref, wg_ref, wv_ref, cw_ref, cb_ref, o_ref, buf_ref, *, tm, halo):
    first = pl.program_id(2) == 0
    h = h_ref[0]
    _push_halo(buf_ref, _dot(h, wg_ref[...]), halo, tm, first)
    gate = _causal_taps(buf_ref, cw_ref, FFN_K, halo, tm) + cb_ref[...]
    o_ref[0] = (gate * jax.nn.sigmoid(gate) * _dot(h, wv_ref[...])).astype(BF16)


def _ffn_up(h2, w_up, conv_w, conv_b, *, tm, tn):
    bn, seq, d = h2.shape
    d_ff = w_up.shape[1] // 2
    nj = d_ff // tn
    halo = SUBLANES
    return pl.pallas_call(
        functools.partial(_ffn_up_kernel, tm=tm, halo=halo),
        grid=(nj, bn, seq // tm),
        in_specs=[pl.BlockSpec((1, tm, d), lambda j, b, i: (b, i, 0)),
                  pl.BlockSpec((d, tn), lambda j, b, i: (0, j)),
                  pl.BlockSpec((d, tn), lambda j, b, i: (0, nj + j)),
                  pl.BlockSpec((FFN_K, tn), lambda j, b, i: (0, j)),
                  pl.BlockSpec((1, tn), lambda j, b, i: (0, j))],
        out_specs=pl.BlockSpec((1, tm, tn), lambda j, b, i: (b, i, j)),
        out_shape=jax.ShapeDtypeStruct((bn, seq, d_ff), BF16),
        scratch_shapes=[pltpu.VMEM((halo + tm, tn), F32)],
        compiler_params=_params(3),
        name="ffn_up_conv",
    )(h2, w_up, w_up, conv_w, conv_b)


def _ffn_down_kernel(a_ref, x_ref, mod_ref, w_ref, g_ref, b_ref, o_ref, *, alpha):
    mod = mod_ref[0]
    y = _dot(a_ref[0], w_ref[...])
    o_ref[0] = _norm_rows(alpha * x_ref[0] + (1.0 + mod[5:6, :]) * y) * g_ref[...] + b_ref[...]


def _ffn_down(act, x1, mod_l, w_down, g, b, *, tm, alpha):
    bn, seq, d = x1.shape
    row = lambda bb, i: (bb, i, 0)
    return pl.pallas_call(
        functools.partial(_ffn_down_kernel, alpha=alpha),
        grid=(bn, seq // tm),
        in_specs=[pl.BlockSpec((1, tm, act.shape[2]), row),
                  pl.BlockSpec((1, tm, d), row),
                  pl.BlockSpec((1, 6, d), lambda bb, i: (bb, 0, 0)),
                  _resident(w_down.shape), _resident(g.shape), _resident(b.shape)],
        out_specs=pl.BlockSpec((1, tm, d), row),
        out_shape=jax.ShapeDtypeStruct((bn, seq, d), F32),
        compiler_params=_params(2),
        name="ffn_down_postln",
    )(act, x1, mod_l, w_down, g, b)


def kernel(x, c, w_mod, b_mod, w_in, gmlp_ln_g, gmlp_ln_b, gmlp_w_s, gmlp_b_s, mla_q_norm, mla_w_uq, mla_kv_norm, mla_w_ukv, sconv_w, conf_w_dw, conf_b_dw, conf_ln_g, conf_ln_b, w_out, post_mix_g, post_mix_b, ffn_w_up, ffn_w_conv, ffn_b_conv, ffn_w_down, post_ffn_g, post_ffn_b):
    bn, seq, d = x.shape
    depth = w_mod.shape[0]
    wg = d // N_GROUPS
    heads = wg // MLA_V
    q_lora, kv_lora = mla_w_uq.shape[1], mla_w_ukv.shape[1]
    alpha = (2.0 * depth) ** 0.25
    t = _tiles(seq)
    row2 = lambda a: a.reshape(1, -1)

    mod = _modulation(c, w_mod, b_mod, t["mod_cols"])
    cs, sn = _rope_tables(seq)

    for l in range(depth):
        w_packed = _pack_inproj_weights(w_in[l], wg, q_lora, kv_lora)
        wuq = _pack_uq_weights(mla_w_uq[l], heads)
        b_s_tile = jnp.repeat(gmlp_b_s[l].T, wg // GMLP_HEADS, axis=1)
        yacd, q, k, vt = _inproj(
            x, mod[l], w_packed, cs, sn, row2(gmlp_ln_g[l]), row2(gmlp_ln_b[l]), gmlp_w_s[l],
            b_s_tile, row2(mla_q_norm[l]), wuq, row2(mla_kv_norm[l]), mla_w_ukv[l].astype(BF16),
            sconv_w[l], conf_w_dw[l], row2(conf_b_dw[l]), row2(conf_ln_g[l]), row2(conf_ln_b[l]),
            tm=t["inproj"], wg=wg, heads=heads)
        yb = _attention(q, k, vt, tq=t["attn"], heads=heads)
        w_o = w_out[l].astype(BF16)
        w_acd = jnp.concatenate([w_o[:wg], w_o[2 * wg:]], axis=0)
        x1, h2 = _outproj(yacd, yb, x, mod[l], w_acd, w_o[wg:2 * wg],
                          row2(post_mix_g[l]), row2(post_mix_b[l]), tm=t["outproj"], alpha=alpha)
        act = _ffn_up(h2, ffn_w_up[l].astype(BF16), ffn_w_conv[l], row2(ffn_b_conv[l]),
                      tm=t["ffn_up"], tn=t["ffn_up_cols"])
        x = _ffn_down(act, x1, mod[l], ffn_w_down[l].astype(BF16), row2(post_ffn_g[l]),
                      row2(post_ffn_b[l]), tm=t["ffn_down"], alpha=alpha)
    return x
```

```python
import functools

import jax
import jax.numpy as jnp
from jax import lax
from jax.experimental import pallas as pl
from jax.experimental.pallas import tpu as pltpu

F32 = jnp.float32
BF16 = jnp.bfloat16

CHUNK = 64
N_GROUPS = 4
GMLP_BLOCK = 128
GMLP_HEADS = 4
MLA_NOPE = 128
MLA_ROPE = 64
MLA_V = 128
ROPE_THETA = 10000.0
SCONV_K = 3
CONF_K = 31
FFN_K = 3
LN_EPS = 1e-5
RMS_EPS = 1e-6

LANES = 128
SUBLANES = 8
MXU_COLS = 256
VMEM_LIMIT_BYTES = 56 * 1024 * 1024

NEG_BIG = -1e30
LOG2_E = 1.4426950408889634


def _tiles(seq):
    def pick(pref):
        t = min(pref, seq)
        assert seq % t == 0
        return t
    return dict(inproj=pick(512), attn=pick(1024), outproj=pick(512),
                ffn_up=pick(512), ffn_up_col_blocks=2, ffn_up_chunk=MXU_COLS,
                ffn_down=pick(256), mod_cols=1024)


def _resident(shape):
    nd = len(shape)
    return pl.BlockSpec(shape, lambda *_: (0,) * nd, pipeline_mode=pl.Buffered(1))


def _params(n_grid):
    return pltpu.CompilerParams(dimension_semantics=("arbitrary",) * n_grid,
                                vmem_limit_bytes=VMEM_LIMIT_BYTES)


def _norm_rows(x):
    mu = jnp.mean(x, axis=-1, keepdims=True)
    xc = x - mu
    var = jnp.mean(xc * xc, axis=-1, keepdims=True)
    return xc * lax.rsqrt(var + LN_EPS)


def _rms_rows(x, gain):
    return x * lax.rsqrt(jnp.mean(x * x, axis=-1, keepdims=True) + RMS_EPS) * gain


def _dot(a, b):
    return jnp.dot(a, b, preferred_element_type=F32)


def _mod_kernel(c_ref, w_ref, b_ref, o_ref):
    c = c_ref[...]
    c_act = c * jax.nn.sigmoid(c)
    o_ref[0] = jnp.dot(c_act, w_ref[0], preferred_element_type=F32,
                       precision=lax.Precision.HIGHEST) + b_ref[0]


def _modulation(c, w_mod, b_mod, tn):
    depth, d, n = w_mod.shape
    bn = c.shape[0]
    rows = -(-bn // SUBLANES) * SUBLANES
    c_pad = jnp.zeros((rows, d), F32).at[:bn].set(c)
    out = pl.pallas_call(
        _mod_kernel,
        grid=(depth, n // tn),
        in_specs=[pl.BlockSpec((rows, d), lambda l, j: (0, 0)),
                  pl.BlockSpec((1, d, tn), lambda l, j: (l, 0, j)),
                  pl.BlockSpec((1, 1, tn), lambda l, j: (l, 0, j))],
        out_specs=pl.BlockSpec((1, rows, tn), lambda l, j: (l, 0, j)),
        out_shape=jax.ShapeDtypeStruct((depth, rows, n), F32),
        compiler_params=_params(2),
        name="modulation",
    )(c_pad, w_mod, b_mod.reshape(depth, 1, n))
    return out[:, :bn].reshape(depth, bn, 6, d)


def _halo_rows(n_taps):
    return -(-(n_taps - 1) // SUBLANES) * SUBLANES


def _causal_conv(y, tail, w, n_taps):
    halo, tm = tail.shape[0], y.shape[0]
    assert halo % SUBLANES == 0 and halo >= n_taps - 1
    ext = jnp.concatenate([tail, y], axis=0)
    acc = None
    for r in range(min(SUBLANES, n_taps)):
        z = ext if r == 0 else pltpu.roll(ext, r, axis=0)
        for a in range((n_taps - 1 - r) // SUBLANES + 1):
            s = a * SUBLANES + r
            lo = halo - a * SUBLANES
            term = z[lo:lo + tm] * w[n_taps - 1 - s:n_taps - s, :]
            acc = term if acc is None else acc + term
    return acc


def _inproj_kernel(x_ref, mod_ref, w_ref, cs_ref, sn_ref,
                   gln_g_ref, gln_b_ref, ws_ref, bs_ref,
                   qn_g_ref, wuq_ref, kvn_g_ref, wukv_ref,
                   sconv_w_ref, cw_ref, cb_ref, cln_g_ref, cln_b_ref,
                   yacd_ref, q_ref, k_ref, vt_ref,
                   stail_ref, ctail_ref, *, tm, wg, q_lora, kv_lora, heads, scale):
    @pl.when(pl.program_id(1) == 0)
    def _():
        stail_ref[...] = jnp.zeros(stail_ref.shape, F32)
        ctail_ref[...] = jnp.zeros(ctail_ref.shape, F32)

    mod = mod_ref[0]
    h = (_norm_rows(x_ref[0]) * (1.0 + mod[1:2, :]) + mod[0:1, :]).astype(BF16)

    o_uv = 0
    o_q = o_uv + 2 * wg
    o_kv = o_q + q_lora
    o_kr = o_kv + kv_lora
    o_s = o_kr + 2 * LANES
    o_ag = o_s + 3 * wg

    ag = _dot(h, w_ref[:, o_ag:o_ag + 2 * wg])
    glu = ag[:, :wg] * jax.nn.sigmoid(ag[:, wg:])
    y = _causal_conv(glu, ctail_ref[...], cw_ref, CONF_K) + cb_ref[...]
    ctail_ref[...] = glu[tm - ctail_ref.shape[0]:, :]
    y = _norm_rows(y) * cln_g_ref[...] + cln_b_ref[...]
    yacd_ref[0, :, 2 * wg:3 * wg] = (y * jax.nn.sigmoid(y)).astype(BF16)

    s3 = _dot(h, w_ref[:, o_s:o_s + 3 * wg])
    gated = s3[:, wg:2 * wg] * s3[:, 2 * wg:]
    y_c = s3[:, :wg] * _causal_conv(gated, stail_ref[...], sconv_w_ref, SCONV_K)
    stail_ref[...] = gated[tm - stail_ref.shape[0]:, :]
    yacd_ref[0, :, wg:2 * wg] = y_c.astype(BF16)

    z = jax.nn.gelu(_dot(h, w_ref[:, o_uv:o_uv + 2 * wg]), approximate=True)
    u = z[:, :wg]
    v = (_norm_rows(z[:, wg:]) * gln_g_ref[...] + gln_b_ref[...]).astype(BF16)
    nblk = tm // GMLP_BLOCK
    hd = wg // GMLP_HEADS
    ii = lax.broadcasted_iota(jnp.int32, (GMLP_BLOCK, GMLP_BLOCK), 0) // CHUNK
    jj = lax.broadcasted_iota(jnp.int32, (GMLP_BLOCK, GMLP_BLOCK), 1) // CHUNK
    bias = bs_ref[...]
    mixed_cols = []
    for hh in range(GMLP_HEADS):
        w_h = jnp.where(jj <= ii, ws_ref[hh], 0.0).astype(BF16)
        v_h = jnp.concatenate(
            [v[n * GMLP_BLOCK:(n + 1) * GMLP_BLOCK, hh * hd:(hh + 1) * hd] for n in range(nblk)],
            axis=1)
        mixed_cols.append(_dot(w_h, v_h))
    for n in range(nblk):
        mixed = jnp.concatenate([m[:, n * hd:(n + 1) * hd] for m in mixed_cols], axis=1) + bias
        rows = slice(n * GMLP_BLOCK, (n + 1) * GMLP_BLOCK)
        yacd_ref[0, rows, 0:wg] = (u[rows, :] * mixed).astype(BF16)

    cs = cs_ref[...]
    sn = sn_ref[...]
    q_lat = _rms_rows(_dot(h, w_ref[:, o_q:o_q + q_lora]), qn_g_ref[...]).astype(BF16)
    qq = _dot(q_lat, wuq_ref[...])
    kv_lat = _rms_rows(_dot(h, w_ref[:, o_kv:o_kv + kv_lora]), kvn_g_ref[...]).astype(BF16)
    kv = _dot(kv_lat, wukv_ref[...])
    kr2 = _dot(h, w_ref[:, o_kr:o_kr + 2 * LANES])
    k_rope = (kr2[:, :LANES] * cs + kr2[:, LANES:] * sn).astype(BF16)
    hw = 2 * LANES
    for hh in range(heads):
        q_nope = qq[:, hh * hw:hh * hw + LANES]
        q_rope = qq[:, hh * hw + LANES:(hh + 1) * hw]
        q_rot = qq[:, heads * hw + hh * LANES:heads * hw + (hh + 1) * LANES]
        q_ref[0, :, hh * hw:hh * hw + LANES] = (q_nope * scale).astype(BF16)
        q_ref[0, :, hh * hw + LANES:(hh + 1) * hw] = ((q_rope * cs + q_rot * sn) * scale).astype(BF16)
        k_ref[0, :, hh * hw:hh * hw + LANES] = kv[:, hh * hw:hh * hw + LANES].astype(BF16)
        k_ref[0, :, hh * hw + LANES:(hh + 1) * hw] = k_rope
        vt_ref[0, 0, hh * LANES:(hh + 1) * LANES, :] = kv[:, hh * hw + LANES:(hh + 1) * hw].T.astype(BF16)


def _pack_inproj_weights(w_in_l, wg, q_lora, kv_lora):
    d = w_in_l.shape[0]
    o = 0
    w_uv = w_in_l[:, o:o + 2 * wg]; o += 2 * wg
    w_q = w_in_l[:, o:o + q_lora]; o += q_lora
    w_kv = w_in_l[:, o:o + kv_lora]; o += kv_lora
    w_kr = w_in_l[:, o:o + MLA_ROPE]; o += MLA_ROPE
    w_s = w_in_l[:, o:o + 3 * wg]; o += 3 * wg
    w_ag = w_in_l[:, o:o + 2 * wg]
    half = MLA_ROPE // 2
    pad = jnp.zeros((d, LANES - MLA_ROPE), w_in_l.dtype)
    w_kr_rot = jnp.concatenate([-w_kr[:, half:], w_kr[:, :half]], axis=1)
    return jnp.concatenate([w_uv, w_q, w_kv, w_kr, pad, w_kr_rot, pad, w_s, w_ag],
                           axis=1).astype(BF16)


def _pack_uq_weights(w_uq_l, heads):
    r = w_uq_l.shape[0]
    half = MLA_ROPE // 2
    w = w_uq_l.reshape(r, heads, MLA_NOPE + MLA_ROPE)
    pad = jnp.zeros((r, heads, LANES - MLA_ROPE), w.dtype)
    rope = w[:, :, MLA_NOPE:]
    main = jnp.concatenate([w[:, :, :MLA_NOPE], rope, pad], axis=2).reshape(r, heads * 2 * LANES)
    rot = jnp.concatenate([-rope[:, :, half:], rope[:, :, :half], pad], axis=2).reshape(r, heads * LANES)
    return jnp.concatenate([main, rot], axis=1).astype(BF16)


def _rope_tables(seq):
    pos = jnp.arange(seq, dtype=F32)
    inv_freq = ROPE_THETA ** (-jnp.arange(0, MLA_ROPE, 2, dtype=F32) / MLA_ROPE)
    ang = pos[:, None] * inv_freq[None, :]
    zeros = jnp.zeros((seq, LANES - MLA_ROPE), F32)
    cos, sin = jnp.cos(ang), jnp.sin(ang)
    return (jnp.concatenate([cos, cos, zeros], axis=1), jnp.concatenate([sin, sin, zeros], axis=1))


def _inproj(x, mod_l, w_packed, cs, sn, gln_g, gln_b, w_s, b_s_tile, qn_g, wuq, kvn_g, wukv,
            sconv_w, conf_w, conf_b, cln_g, cln_b, *, tm, wg, heads):
    bn, seq, d = x.shape
    q_lora, kv_lora = wuq.shape[0], wukv.shape[0]
    s_halo = _halo_rows(SCONV_K)
    c_halo = _halo_rows(CONF_K)
    kern = functools.partial(
        _inproj_kernel, tm=tm, wg=wg, q_lora=q_lora, kv_lora=kv_lora, heads=heads,
        scale=float((MLA_NOPE + MLA_ROPE) ** -0.5 * LOG2_E))
    row = lambda b, i: (b, i, 0)
    smalls = [gln_g, gln_b, w_s, b_s_tile, qn_g, wuq, kvn_g, wukv, sconv_w, conf_w, conf_b,
              cln_g, cln_b]
    return pl.pallas_call(
        kern,
        grid=(bn, seq // tm),
        in_specs=[pl.BlockSpec((1, tm, d), row),
                  pl.BlockSpec((1, 6, d), lambda b, i: (b, 0, 0)),
                  _resident(w_packed.shape),
                  pl.BlockSpec((tm, LANES), lambda b, i: (i, 0)),
                  pl.BlockSpec((tm, LANES), lambda b, i: (i, 0))]
                 + [_resident(a.shape) for a in smalls],
        out_specs=[pl.BlockSpec((1, tm, 3 * wg), row),
                   pl.BlockSpec((1, tm, heads * 2 * LANES), row),
                   pl.BlockSpec((1, tm, heads * 2 * LANES), row),
                   pl.BlockSpec((1, 1, heads * LANES, tm), lambda b, i: (b, i, 0, 0))],
        out_shape=[jax.ShapeDtypeStruct((bn, seq, 3 * wg), BF16),
                   jax.ShapeDtypeStruct((bn, seq, heads * 2 * LANES), BF16),
                   jax.ShapeDtypeStruct((bn, seq, heads * 2 * LANES), BF16),
                   jax.ShapeDtypeStruct((bn, seq // tm, heads * LANES, tm), BF16)],
        scratch_shapes=[pltpu.VMEM((s_halo, wg), F32), pltpu.VMEM((c_halo, wg), F32)],
        compiler_params=_params(2),
        name="inproj_mixers",
    )(x, mod_l, w_packed, cs, sn, *smalls)


def _attn_kernel(q_ref, k_ref, vt_ref, o_ref, qt_ref, m_ref, acc_ref, *, n_sub, tk):
    qi = pl.program_id(2)
    qt_ref[...] = q_ref[0].astype(F32).T.astype(BF16)
    m_ref[...] = jnp.full(m_ref.shape, NEG_BIG, F32)
    acc_ref[...] = jnp.zeros(acc_ref.shape, F32)
    ones = jnp.ones((acc_ref.shape[1] - MLA_V, tk), BF16)

    def load_keys(j):
        kb = k_ref[0, pl.ds(pl.multiple_of(j * tk, tk), tk), :]
        return kb, jnp.concatenate([vt_ref[0, j], ones], axis=0)

    def step(sub, kb, vtb, masked):
        st = _dot(kb, qt_ref[:, sub * tk:(sub + 1) * tk])
        if masked:
            kc = lax.broadcasted_iota(jnp.int32, (tk, tk), 0) // CHUNK
            qc = lax.broadcasted_iota(jnp.int32, (tk, tk), 1) // CHUNK
            st = jnp.where(kc <= qc, st, NEG_BIG)
        m_prev = m_ref[sub]
        m_new = jnp.maximum(m_prev, jnp.max(st, axis=0, keepdims=True))
        p = jnp.exp2(st - m_new).astype(BF16)
        acc_ref[sub] = jnp.exp2(m_prev - m_new) * acc_ref[sub] + _dot(vtb, p)
        m_ref[sub] = m_new

    def body(j, carry):
        kb, vtb = load_keys(j)
        for sub in range(n_sub):
            step(sub, kb, vtb, masked=False)
        return carry

    lax.fori_loop(0, n_sub * qi, body, 0)
    for e in range(n_sub):
        kb, vtb = load_keys(n_sub * qi + e)
        for sub in range(e, n_sub):
            step(sub, kb, vtb, masked=(sub == e))
    for sub in range(n_sub):
        acc = acc_ref[sub]
        o_ref[0, sub * tk:(sub + 1) * tk, :] = (acc[:MLA_V] / acc[MLA_V:MLA_V + 1]).T.astype(BF16)


def _attention(q, k, vt, *, tq, heads):
    bn, seq, _ = q.shape
    tk = vt.shape[3]
    n_sub = tq // tk
    assert tq == n_sub * tk and n_sub >= 1
    ones_rows = 2 * SUBLANES
    kern = functools.partial(_attn_kernel, n_sub=n_sub, tk=tk)
    return pl.pallas_call(
        kern,
        grid=(bn, heads, seq // tq),
        in_specs=[pl.BlockSpec((1, tq, 2 * LANES), lambda b, h, i: (b, i, h)),
                  pl.BlockSpec((1, seq, 2 * LANES), lambda b, h, i: (b, 0, h)),
                  pl.BlockSpec((1, seq // tk, LANES, tk), lambda b, h, i: (b, 0, h, 0))],
        out_specs=pl.BlockSpec((1, tq, LANES), lambda b, h, i: (b, i, h)),
        out_shape=jax.ShapeDtypeStruct((bn, seq, heads * LANES), BF16),
        scratch_shapes=[pltpu.VMEM((2 * LANES, tq), BF16),
                        pltpu.VMEM((n_sub, 1, tk), F32),
                        pltpu.VMEM((n_sub, MLA_V + ones_rows, tk), F32)],
        compiler_params=_params(3),
        name="mla_attention",
    )(q, k, vt)


def _outproj_kernel(yacd_ref, yb_ref, x_ref, mod_ref, wacd_ref, wb_ref, g_ref, b_ref,
                    x1_ref, h2_ref, *, alpha):
    mod = mod_ref[0]
    y = _dot(yacd_ref[0], wacd_ref[...]) + _dot(yb_ref[0], wb_ref[...])
    x1 = _norm_rows(alpha * x_ref[0] + (1.0 + mod[2:3, :]) * y) * g_ref[...] + b_ref[...]
    x1_ref[0] = x1
    h2_ref[0] = (_norm_rows(x1) * (1.0 + mod[4:5, :]) + mod[3:4, :]).astype(BF16)


def _outproj(yacd, yb, x, mod_l, w_acd, w_b, g, b, *, tm, alpha):
    bn, seq, d = x.shape
    row = lambda bb, i: (bb, i, 0)
    return pl.pallas_call(
        functools.partial(_outproj_kernel, alpha=alpha),
        grid=(bn, seq // tm),
        in_specs=[pl.BlockSpec((1, tm, yacd.shape[2]), row),
                  pl.BlockSpec((1, tm, yb.shape[2]), row),
                  pl.BlockSpec((1, tm, d), row),
                  pl.BlockSpec((1, 6, d), lambda bb, i: (bb, 0, 0)),
                  _resident(w_acd.shape), _resident(w_b.shape),
                  _resident(g.shape), _resident(b.shape)],
        out_specs=[pl.BlockSpec((1, tm, d), row), pl.BlockSpec((1, tm, d), row)],
        out_shape=[jax.ShapeDtypeStruct((bn, seq, d), F32),
                   jax.ShapeDtypeStruct((bn, seq, d), BF16)],
        compiler_params=_params(2),
        name="outproj_postln",
    )(yacd, yb, x, mod_l, w_acd, w_b, g, b)


def _ffn_up_kernel(h_ref, wg_ref, wv_ref, cw_ref, cb_ref, o_ref, tail_ref, *, tm, tc):
    @pl.when(pl.program_id(2) == 0)
    def _():
        tail_ref[...] = jnp.zeros(tail_ref.shape, F32)

    h = h_ref[0]
    halo = tail_ref.shape[0]
    for c in range(wg_ref.shape[1] // tc):
        cols = slice(c * tc, (c + 1) * tc)
        gate = _dot(h, wg_ref[:, cols])
        conv = _causal_conv(gate, tail_ref[:, cols], cw_ref.at[:, cols], FFN_K) + cb_ref[:, cols]
        tail_ref[:, cols] = gate[tm - halo:, :]
        o_ref[0, :, cols] = (conv * jax.nn.sigmoid(conv) * _dot(h, wv_ref[:, cols])).astype(BF16)


def _ffn_up(h2, w_up, conv_w, conv_b, *, tm, tn, tc):
    bn, seq, d = h2.shape
    d_ff = w_up.shape[1] // 2
    nj = d_ff // tn
    assert d_ff == nj * tn and tn % tc == 0
    once = pl.Buffered(1)
    return pl.pallas_call(
        functools.partial(_ffn_up_kernel, tm=tm, tc=tc),
        grid=(nj, bn, seq // tm),
        in_specs=[pl.BlockSpec((1, tm, d), lambda j, b, i: (b, i, 0)),
                  pl.BlockSpec((d, tn), lambda j, b, i: (0, j), pipeline_mode=once),
                  pl.BlockSpec((d, tn), lambda j, b, i: (0, nj + j), pipeline_mode=once),
                  pl.BlockSpec((FFN_K, tn), lambda j, b, i: (0, j)),
                  pl.BlockSpec((1, tn), lambda j, b, i: (0, j))],
        out_specs=pl.BlockSpec((1, tm, tn), lambda j, b, i: (b, i, j)),
        out_shape=jax.ShapeDtypeStruct((bn, seq, d_ff), BF16),
        scratch_shapes=[pltpu.VMEM((_halo_rows(FFN_K), tn), F32)],
        compiler_params=_params(3),
        name="ffn_up_conv",
    )(h2, w_up, w_up, conv_w, conv_b)


def _ffn_down_kernel(a_ref, x_ref, mod_ref, w_ref, g_ref, b_ref, o_ref, *, alpha):
    mod = mod_ref[0]
    y = _dot(a_ref[0], w_ref[...])
    o_ref[0] = _norm_rows(alpha * x_ref[0] + (1.0 + mod[5:6, :]) * y) * g_ref[...] + b_ref[...]


def _ffn_down(act, x1, mod_l, w_down, g, b, *, tm, alpha):
    bn, seq, d = x1.shape
    row = lambda bb, i: (bb, i, 0)
    return pl.pallas_call(
        functools.partial(_ffn_down_kernel, alpha=alpha),
        grid=(bn, seq // tm),
        in_specs=[pl.BlockSpec((1, tm, act.shape[2]), row),
                  pl.BlockSpec((1, tm, d), row),
                  pl.BlockSpec((1, 6, d), lambda bb, i: (bb, 0, 0)),
                  _resident(w_down.shape), _resident(g.shape), _resident(b.shape)],
        out_specs=pl.BlockSpec((1, tm, d), row),
        out_shape=jax.ShapeDtypeStruct((bn, seq, d), F32),
        compiler_params=_params(2),
        name="ffn_down_postln",
    )(act, x1, mod_l, w_down, g, b)


def kernel(x, c, w_mod, b_mod, w_in, gmlp_ln_g, gmlp_ln_b, gmlp_w_s, gmlp_b_s, mla_q_norm, mla_w_uq, mla_kv_norm, mla_w_ukv, sconv_w, conf_w_dw, conf_b_dw, conf_ln_g, conf_ln_b, w_out, post_mix_g, post_mix_b, ffn_w_up, ffn_w_conv, ffn_b_conv, ffn_w_down, post_ffn_g, post_ffn_b):
    bn, seq, d = x.shape
    depth = w_mod.shape[0]
    wg = d // N_GROUPS
    heads = wg // MLA_V
    q_lora, kv_lora = mla_w_uq.shape[1], mla_w_ukv.shape[1]
    alpha = (2.0 * depth) ** 0.25
    t = _tiles(seq)
    row2 = lambda a: a.reshape(1, -1)

    mod = _modulation(c, w_mod, b_mod, t["mod_cols"])
    cs, sn = _rope_tables(seq)

    for l in range(depth):
        w_packed = _pack_inproj_weights(w_in[l], wg, q_lora, kv_lora)
        wuq = _pack_uq_weights(mla_w_uq[l], heads)
        b_s_tile = jnp.repeat(gmlp_b_s[l].T, wg // GMLP_HEADS, axis=1)
        yacd, q, k, vt = _inproj(
            x, mod[l], w_packed, cs, sn, row2(gmlp_ln_g[l]), row2(gmlp_ln_b[l]), gmlp_w_s[l],
            b_s_tile, row2(mla_q_norm[l]), wuq, row2(mla_kv_norm[l]), mla_w_ukv[l].astype(BF16),
            sconv_w[l], conf_w_dw[l], row2(conf_b_dw[l]), row2(conf_ln_g[l]), row2(conf_ln_b[l]),
            tm=t["inproj"], wg=wg, heads=heads)
        yb = _attention(q, k, vt, tq=t["attn"], heads=heads)
        w_o = w_out[l].astype(BF16)
        w_acd = jnp.concatenate([w_o[:wg], w_o[2 * wg:]], axis=0)
        x1, h2 = _outproj(yacd, yb, x, mod[l], w_acd, w_o[wg:2 * wg],
                          row2(post_mix_g[l]), row2(post_mix_b[l]), tm=t["outproj"], alpha=alpha)
        act = _ffn_up(h2, ffn_w_up[l].astype(BF16), ffn_w_conv[l], row2(ffn_b_conv[l]),
                      tm=t["ffn_up"], tn=ffn_w_conv.shape[2] // t["ffn_up_col_blocks"],
                      tc=t["ffn_up_chunk"])
        x = _ffn_down(act, x1, mod[l], ffn_w_down[l].astype(BF16), row2(post_ffn_g[l]),
                      row2(post_ffn_b[l]), tm=t["ffn_down"], alpha=alpha)
    return x
```

```python
import functools

import jax
import jax.numpy as jnp
from jax import lax
from jax.experimental import pallas as pl
from jax.experimental.pallas import tpu as pltpu

F32 = jnp.float32
BF16 = jnp.bfloat16

CHUNK = 64
N_GROUPS = 4
GMLP_BLOCK = 128
GMLP_HEADS = 4
MLA_NOPE = 128
MLA_ROPE = 64
MLA_V = 128
ROPE_THETA = 10000.0
SCONV_K = 3
CONF_K = 31
FFN_K = 3
LN_EPS = 1e-5
RMS_EPS = 1e-6

LANES = 128
SUBLANES = 8
MXU_COLS = 256
VMEM_LIMIT_BYTES = 56 * 1024 * 1024

NEG_BIG = -1e30
LOG2_E = 1.4426950408889634


def _tiles(seq):
    def pick(pref):
        t = min(pref, seq)
        assert seq % t == 0
        return t
    return dict(inproj=pick(512), attn=pick(1024), outproj=pick(512),
                ffn_up=pick(512), ffn_up_col_blocks=2, ffn_up_chunk=MXU_COLS,
                ffn_down=pick(256), mod_cols=1024)


def _resident(shape):
    nd = len(shape)
    return pl.BlockSpec(shape, lambda *_: (0,) * nd, pipeline_mode=pl.Buffered(1))


def _params(n_grid):
    return pltpu.CompilerParams(dimension_semantics=("arbitrary",) * n_grid,
                                vmem_limit_bytes=VMEM_LIMIT_BYTES)


def _norm_rows(x):
    mu = jnp.mean(x, axis=-1, keepdims=True)
    xc = x - mu
    var = jnp.mean(xc * xc, axis=-1, keepdims=True)
    return xc * lax.rsqrt(var + LN_EPS)


def _rms_rows(x, gain):
    return x * lax.rsqrt(jnp.mean(x * x, axis=-1, keepdims=True) + RMS_EPS) * gain


def _dot(a, b):
    return jnp.dot(a, b, preferred_element_type=F32)


def _mod_kernel(c_ref, w_ref, b_ref, o_ref):
    c = c_ref[...]
    c_act = c * jax.nn.sigmoid(c)
    o_ref[0] = jnp.dot(c_act, w_ref[0], preferred_element_type=F32,
                       precision=lax.Precision.HIGHEST) + b_ref[0]


def _modulation(c, w_mod, b_mod, tn):
    depth, d, n = w_mod.shape
    bn = c.shape[0]
    rows = -(-bn // SUBLANES) * SUBLANES
    c_pad = jnp.zeros((rows, d), F32).at[:bn].set(c)
    out = pl.pallas_call(
        _mod_kernel,
        grid=(depth, n // tn),
        in_specs=[pl.BlockSpec((rows, d), lambda l, j: (0, 0)),
                  pl.BlockSpec((1, d, tn), lambda l, j: (l, 0, j)),
                  pl.BlockSpec((1, 1, tn), lambda l, j: (l, 0, j))],
        out_specs=pl.BlockSpec((1, rows, tn), lambda l, j: (l, 0, j)),
        out_shape=jax.ShapeDtypeStruct((depth, rows, n), F32),
        compiler_params=_params(2),
        name="modulation",
    )(c_pad, w_mod, b_mod.reshape(depth, 1, n))
    return out[:, :bn].reshape(depth, bn, 6, d)


def _halo_rows(n_taps):
    return -(-(n_taps - 1) // SUBLANES) * SUBLANES


def _causal_conv(y, tail, w, n_taps):
    halo, tm = tail.shape[0], y.shape[0]
    assert halo % SUBLANES == 0 and halo >= n_taps - 1
    ext = jnp.concatenate([tail, y], axis=0)
    acc = None
    for r in range(min(SUBLANES, n_taps)):
        z = ext if r == 0 else pltpu.roll(ext, r, axis=0)
        for a in range((n_taps - 1 - r) // SUBLANES + 1):
            s = a * SUBLANES + r
            lo = halo - a * SUBLANES
            term = z[lo:lo + tm] * w[n_taps - 1 - s:n_taps - s, :]
            acc = term if acc is None else acc + term
    return acc


def _inproj_kernel(x_ref, mod_ref, w_ref, cs_ref, sn_ref,
                   gln_g_ref, gln_b_ref, ws_ref, bs_ref,
                   qn_g_ref, wuq_ref, kvn_g_ref, wukv_ref,
                   sconv_w_ref, cw_ref, cb_ref, cln_g_ref, cln_b_ref,
                   yacd_ref, q_ref, k_ref, vt_ref,
                   stail_ref, ctail_ref, *, tm, wg, q_lora, kv_lora, heads, scale):
    @pl.when(pl.program_id(1) == 0)
    def _():
        stail_ref[...] = jnp.zeros(stail_ref.shape, F32)
        ctail_ref[...] = jnp.zeros(ctail_ref.shape, F32)

    mod = mod_ref[0]
    h = (_norm_rows(x_ref[0]) * (1.0 + mod[1:2, :]) + mod[0:1, :]).astype(BF16)

    o_uv = 0
    o_q = o_uv + 2 * wg
    o_kv = o_q + q_lora
    o_kr = o_kv + kv_lora
    o_s = o_kr + 2 * LANES
    o_ag = o_s + 3 * wg

    ag = _dot(h, w_ref[:, o_ag:o_ag + 2 * wg])
    glu = ag[:, :wg] * jax.nn.sigmoid(ag[:, wg:])
    y = _causal_conv(glu, ctail_ref[...], cw_ref, CONF_K) + cb_ref[...]
    ctail_ref[...] = glu[tm - ctail_ref.shape[0]:, :]
    y = _norm_rows(y) * cln_g_ref[...] + cln_b_ref[...]
    yacd_ref[0, :, 2 * wg:3 * wg] = (y * jax.nn.sigmoid(y)).astype(BF16)

    s3 = _dot(h, w_ref[:, o_s:o_s + 3 * wg])
    gated = s3[:, wg:2 * wg] * s3[:, 2 * wg:]
    y_c = s3[:, :wg] * _causal_conv(gated, stail_ref[...], sconv_w_ref, SCONV_K)
    stail_ref[...] = gated[tm - stail_ref.shape[0]:, :]
    yacd_ref[0, :, wg:2 * wg] = y_c.astype(BF16)

    z = jax.nn.gelu(_dot(h, w_ref[:, o_uv:o_uv + 2 * wg]), approximate=True)
    u = z[:, :wg]
    v = (_norm_rows(z[:, wg:]) * gln_g_ref[...] + gln_b_ref[...]).astype(BF16)
    nblk = tm // GMLP_BLOCK
    hd = wg // GMLP_HEADS
    ii = lax.broadcasted_iota(jnp.int32, (GMLP_BLOCK, GMLP_BLOCK), 0) // CHUNK
    jj = lax.broadcasted_iota(jnp.int32, (GMLP_BLOCK, GMLP_BLOCK), 1) // CHUNK
    bias = bs_ref[...]
    mixed_cols = []
    for hh in range(GMLP_HEADS):
        w_h = jnp.where(jj <= ii, ws_ref[hh], 0.0).astype(BF16)
        v_h = jnp.concatenate(
            [v[n * GMLP_BLOCK:(n + 1) * GMLP_BLOCK, hh * hd:(hh + 1) * hd] for n in range(nblk)],
            axis=1)
        mixed_cols.append(_dot(w_h, v_h))
    for n in range(nblk):
        mixed = jnp.concatenate([m[:, n * hd:(n + 1) * hd] for m in mixed_cols], axis=1) + bias
        rows = slice(n * GMLP_BLOCK, (n + 1) * GMLP_BLOCK)
        yacd_ref[0, rows, 0:wg] = (u[rows, :] * mixed).astype(BF16)

    cs = cs_ref[...]
    sn = sn_ref[...]
    q_lat = _rms_rows(_dot(h, w_ref[:, o_q:o_q + q_lora]), qn_g_ref[...]).astype(BF16)
    qq = _dot(q_lat, wuq_ref[...])
    kv_lat = _rms_rows(_dot(h, w_ref[:, o_kv:o_kv + kv_lora]), kvn_g_ref[...]).astype(BF16)
    kv = _dot(kv_lat, wukv_ref[...])
    kr2 = _dot(h, w_ref[:, o_kr:o_kr + 2 * LANES])
    k_rope = (kr2[:, :LANES] * cs + kr2[:, LANES:] * sn).astype(BF16)
    hw = 2 * LANES
    for hh in range(heads):
        q_nope = qq[:, hh * hw:hh * hw + LANES]
        q_rope = qq[:, hh * hw + LANES:(hh + 1) * hw]
        q_rot = qq[:, heads * hw + hh * LANES:heads * hw + (hh + 1) * LANES]
        q_ref[0, :, hh * hw:hh * hw + LANES] = (q_nope * scale).astype(BF16)
        q_ref[0, :, hh * hw + LANES:(hh + 1) * hw] = ((q_rope * cs + q_rot * sn) * scale).astype(BF16)
        k_ref[0, :, hh * hw:hh * hw + LANES] = kv[:, hh * hw:hh * hw + LANES].astype(BF16)
        k_ref[0, :, hh * hw + LANES:(hh + 1) * hw] = k_rope
        vt_ref[0, 0, hh * LANES:(hh + 1) * LANES, :] = kv[:, hh * hw + LANES:(hh + 1) * hw].T.astype(BF16)


def _pack_inproj_weights(w_in_l, wg, q_lora, kv_lora):
    d = w_in_l.shape[0]
    o = 0
    w_uv = w_in_l[:, o:o + 2 * wg]; o += 2 * wg
    w_q = w_in_l[:, o:o + q_lora]; o += q_lora
    w_kv = w_in_l[:, o:o + kv_lora]; o += kv_lora
    w_kr = w_in_l[:, o:o + MLA_ROPE]; o += MLA_ROPE
    w_s = w_in_l[:, o:o + 3 * wg]; o += 3 * wg
    w_ag = w_in_l[:, o:o + 2 * wg]
    half = MLA_ROPE // 2
    pad = jnp.zeros((d, LANES - MLA_ROPE), w_in_l.dtype)
    w_kr_rot = jnp.concatenate([-w_kr[:, half:], w_kr[:, :half]], axis=1)
    return jnp.concatenate([w_uv, w_q, w_kv, w_kr, pad, w_kr_rot, pad, w_s, w_ag],
                           axis=1).astype(BF16)


def _pack_uq_weights(w_uq_l, heads):
    r = w_uq_l.shape[0]
    half = MLA_ROPE // 2
    w = w_uq_l.reshape(r, heads, MLA_NOPE + MLA_ROPE)
    pad = jnp.zeros((r, heads, LANES - MLA_ROPE), w.dtype)
    rope = w[:, :, MLA_NOPE:]
    main = jnp.concatenate([w[:, :, :MLA_NOPE], rope, pad], axis=2).reshape(r, heads * 2 * LANES)
    rot = jnp.concatenate([-rope[:, :, half:], rope[:, :, :half], pad], axis=2).reshape(r, heads * LANES)
    return jnp.concatenate([main, rot], axis=1).astype(BF16)


def _rope_tables(seq):
    pos = jnp.arange(seq, dtype=F32)
    inv_freq = ROPE_THETA ** (-jnp.arange(0, MLA_ROPE, 2, dtype=F32) / MLA_ROPE)
    ang = pos[:, None] * inv_freq[None, :]
    zeros = jnp.zeros((seq, LANES - MLA_ROPE), F32)
    cos, sin = jnp.cos(ang), jnp.sin(ang)
    return (jnp.concatenate([cos, cos, zeros], axis=1), jnp.concatenate([sin, sin, zeros], axis=1))


def _inproj(x, mod_l, w_packed, cs, sn, gln_g, gln_b, w_s, b_s_tile, qn_g, wuq, kvn_g, wukv,
            sconv_w, conf_w, conf_b, cln_g, cln_b, *, tm, wg, heads):
    bn, seq, d = x.shape
    q_lora, kv_lora = wuq.shape[0], wukv.shape[0]
    s_halo = _halo_rows(SCONV_K)
    c_halo = _halo_rows(CONF_K)
    kern = functools.partial(
        _inproj_kernel, tm=tm, wg=wg, q_lora=q_lora, kv_lora=kv_lora, heads=heads,
        scale=float((MLA_NOPE + MLA_ROPE) ** -0.5 * LOG2_E))
    row = lambda b, i: (b, i, 0)
    smalls = [gln_g, gln_b, w_s, b_s_tile, qn_g, wuq, kvn_g, wukv, sconv_w, conf_w, conf_b,
              cln_g, cln_b]
    return pl.pallas_call(
        kern,
        grid=(bn, seq // tm),
        in_specs=[pl.BlockSpec((1, tm, d), row),
                  pl.BlockSpec((1, 6, d), lambda b, i: (b, 0, 0)),
                  _resident(w_packed.shape),
                  pl.BlockSpec((tm, LANES), lambda b, i: (i, 0)),
                  pl.BlockSpec((tm, LANES), lambda b, i: (i, 0))]
                 + [_resident(a.shape) for a in smalls],
        out_specs=[pl.BlockSpec((1, tm, 3 * wg), row),
                   pl.BlockSpec((1, tm, heads * 2 * LANES), row),
                   pl.BlockSpec((1, tm, heads * 2 * LANES), row),
                   pl.BlockSpec((1, 1, heads * LANES, tm), lambda b, i: (b, i, 0, 0))],
        out_shape=[jax.ShapeDtypeStruct((bn, seq, 3 * wg), BF16),
                   jax.ShapeDtypeStruct((bn, seq, heads * 2 * LANES), BF16),
                   jax.ShapeDtypeStruct((bn, seq, heads * 2 * LANES), BF16),
                   jax.ShapeDtypeStruct((bn, seq // tm, heads * LANES, tm), BF16)],
        scratch_shapes=[pltpu.VMEM((s_halo, wg), F32), pltpu.VMEM((c_halo, wg), F32)],
        compiler_params=_params(2),
        name="inproj_mixers",
    )(x, mod_l, w_packed, cs, sn, *smalls)


def _attn_kernel(q_ref, k_ref, vt_ref, o_ref, qt_ref, s_ref, mc_ref, m_ref, acc_ref, *, n_sub, tk):
    assert n_sub == 2
    qi = pl.program_id(2)
    first_diag = n_sub * qi
    qt_ref[...] = q_ref[0].astype(F32).T.astype(BF16)
    m_ref[...] = jnp.full(m_ref.shape, NEG_BIG, F32)
    acc_ref[...] = jnp.zeros(acc_ref.shape, F32)
    ones = jnp.ones((acc_ref.shape[1] - MLA_V, tk), BF16)
    all_subs = tuple(range(n_sub))

    def scores(j, slot, subs):
        kb = k_ref[0, pl.ds(pl.multiple_of(j * tk, tk), tk), :]
        for sub in subs:
            st = _dot(kb, qt_ref[:, sub * tk:(sub + 1) * tk])
            s_ref[slot, sub] = st
            mc_ref[slot, sub] = jnp.max(st, axis=0, keepdims=True)

    def update(j, slot, subs, diag_sub=None):
        vtb = jnp.concatenate([vt_ref[0, j], ones], axis=0)
        for sub in subs:
            st = s_ref[slot, sub]
            if sub == diag_sub:
                kc = lax.broadcasted_iota(jnp.int32, (tk, tk), 0) // CHUNK
                qc = lax.broadcasted_iota(jnp.int32, (tk, tk), 1) // CHUNK
                st = jnp.where(kc <= qc, st, NEG_BIG)
                m_cur = jnp.max(st, axis=0, keepdims=True)
            else:
                m_cur = mc_ref[slot, sub]
            m_prev = m_ref[sub]
            m_new = jnp.maximum(m_prev, m_cur)
            p = jnp.exp2(st - m_new).astype(BF16)
            acc_ref[sub] = jnp.exp2(m_prev - m_new) * acc_ref[sub] + _dot(vtb, p)
            m_ref[sub] = m_new

    scores(0, 0, all_subs)

    def body(jj, carry):
        j = n_sub * jj
        scores(j + 1, 1, all_subs)
        update(j, 0, all_subs)
        scores(j + 2, 0, all_subs)
        update(j + 1, 1, all_subs)
        return carry

    lax.fori_loop(0, qi, body, 0)
    scores(first_diag + 1, 1, (1,))
    update(first_diag, 0, all_subs, diag_sub=0)
    update(first_diag + 1, 1, (1,), diag_sub=1)
    for sub in all_subs:
        acc = acc_ref[sub]
        o_ref[0, sub * tk:(sub + 1) * tk, :] = (acc[:MLA_V] / acc[MLA_V:MLA_V + 1]).T.astype(BF16)


def _attention(q, k, vt, *, tq, heads):
    bn, seq, _ = q.shape
    tk = vt.shape[3]
    n_sub = tq // tk
    assert tq == n_sub * tk and n_sub >= 1
    ones_rows = 2 * SUBLANES
    kern = functools.partial(_attn_kernel, n_sub=n_sub, tk=tk)
    return pl.pallas_call(
        kern,
        grid=(bn, heads, seq // tq),
        in_specs=[pl.BlockSpec((1, tq, 2 * LANES), lambda b, h, i: (b, i, h)),
                  pl.BlockSpec((1, seq, 2 * LANES), lambda b, h, i: (b, 0, h)),
                  pl.BlockSpec((1, seq // tk, LANES, tk), lambda b, h, i: (b, 0, h, 0))],
        out_specs=pl.BlockSpec((1, tq, LANES), lambda b, h, i: (b, i, h)),
        out_shape=jax.ShapeDtypeStruct((bn, seq, heads * LANES), BF16),
        scratch_shapes=[pltpu.VMEM((2 * LANES, tq), BF16),
                        pltpu.VMEM((2, n_sub, tk, tk), F32),
                        pltpu.VMEM((2, n_sub, 1, tk), F32),
                        pltpu.VMEM((n_sub, 1, tk), F32),
                        pltpu.VMEM((n_sub, MLA_V + ones_rows, tk), F32)],
        compiler_params=_params(3),
        name="mla_attention",
    )(q, k, vt)


def _outproj_kernel(yacd_ref, yb_ref, x_ref, mod_ref, wacd_ref, wb_ref, g_ref, b_ref,
                    x1_ref, h2_ref, *, alpha, sub_rows):
    mod = mod_ref[0]
    for r in range(x_ref.shape[1] // sub_rows):
        rows = slice(r * sub_rows, (r + 1) * sub_rows)
        y = _dot(yacd_ref[0, rows, :], wacd_ref[...]) + _dot(yb_ref[0, rows, :], wb_ref[...])
        x1 = _norm_rows(alpha * x_ref[0, rows, :] + (1.0 + mod[2:3, :]) * y) * g_ref[...] + b_ref[...]
        x1_ref[0, rows, :] = x1
        h2_ref[0, rows, :] = (_norm_rows(x1) * (1.0 + mod[4:5, :]) + mod[3:4, :]).astype(BF16)


def _outproj(yacd, yb, x, mod_l, w_acd, w_b, g, b, *, tm, alpha):
    bn, seq, d = x.shape
    row = lambda bb, i: (bb, i, 0)
    return pl.pallas_call(
        functools.partial(_outproj_kernel, alpha=alpha, sub_rows=min(tm, MXU_COLS)),
        grid=(bn, seq // tm),
        in_specs=[pl.BlockSpec((1, tm, yacd.shape[2]), row),
                  pl.BlockSpec((1, tm, yb.shape[2]), row),
                  pl.BlockSpec((1, tm, d), row),
                  pl.BlockSpec((1, 6, d), lambda bb, i: (bb, 0, 0)),
                  _resident(w_acd.shape), _resident(w_b.shape),
                  _resident(g.shape), _resident(b.shape)],
        out_specs=[pl.BlockSpec((1, tm, d), row), pl.BlockSpec((1, tm, d), row)],
        out_shape=[jax.ShapeDtypeStruct((bn, seq, d), F32),
                   jax.ShapeDtypeStruct((bn, seq, d), BF16)],
        compiler_params=_params(2),
        name="outproj_postln",
    )(yacd, yb, x, mod_l, w_acd, w_b, g, b)


def _ffn_up_kernel(h_ref, wg_ref, wv_ref, cw_ref, cb_ref, o_ref, tail_ref, *, tm, tc):
    @pl.when(pl.program_id(2) == 0)
    def _():
        tail_ref[...] = jnp.zeros(tail_ref.shape, F32)

    h = h_ref[0]
    halo = tail_ref.shape[0]
    for c in range(wg_ref.shape[1] // tc):
        cols = slice(c * tc, (c + 1) * tc)
        gate = _dot(h, wg_ref[:, cols])
        conv = _causal_conv(gate, tail_ref[:, cols], cw_ref.at[:, cols], FFN_K) + cb_ref[:, cols]
        tail_ref[:, cols] = gate[tm - halo:, :]
        o_ref[0, :, cols] = (conv * jax.nn.sigmoid(conv) * _dot(h, wv_ref[:, cols])).astype(BF16)


def _ffn_up(h2, w_up, conv_w, conv_b, *, tm, tn, tc):
    bn, seq, d = h2.shape
    d_ff = w_up.shape[1] // 2
    nj = d_ff // tn
    assert d_ff == nj * tn and tn % tc == 0
    once = pl.Buffered(1)
    return pl.pallas_call(
        functools.partial(_ffn_up_kernel, tm=tm, tc=tc),
        grid=(nj, bn, seq // tm),
        in_specs=[pl.BlockSpec((1, tm, d), lambda j, b, i: (b, i, 0)),
                  pl.BlockSpec((d, tn), lambda j, b, i: (0, j), pipeline_mode=once),
                  pl.BlockSpec((d, tn), lambda j, b, i: (0, nj + j), pipeline_mode=once),
                  pl.BlockSpec((FFN_K, tn), lambda j, b, i: (0, j)),
                  pl.BlockSpec((1, tn), lambda j, b, i: (0, j))],
        out_specs=pl.BlockSpec((1, tm, tn), lambda j, b, i: (b, i, j)),
        out_shape=jax.ShapeDtypeStruct((bn, seq, d_ff), BF16),
        scratch_shapes=[pltpu.VMEM((_halo_rows(FFN_K), tn), F32)],
        compiler_params=_params(3),
        name="ffn_up_conv",
    )(h2, w_up, w_up, conv_w, conv_b)


def _ffn_down_kernel(a_ref, x_ref, mod_ref, w_ref, g_ref, b_ref, o_ref, *, alpha):
    mod = mod_ref[0]
    y = _dot(a_ref[0], w_ref[...])
    o_ref[0] = _norm_rows(alpha * x_ref[0] + (1.0 + mod[5:6, :]) * y) * g_ref[...] + b_ref[...]


def _ffn_down(act, x1, mod_l, w_down, g, b, *, tm, alpha):
    bn, seq, d = x1.shape
    row = lambda bb, i: (bb, i, 0)
    return pl.pallas_call(
        functools.partial(_ffn_down_kernel, alpha=alpha),
        grid=(bn, seq // tm),
        in_specs=[pl.BlockSpec((1, tm, act.shape[2]), row),
                  pl.BlockSpec((1, tm, d), row),
                  pl.BlockSpec((1, 6, d), lambda bb, i: (bb, 0, 0)),
                  _resident(w_down.shape), _resident(g.shape), _resident(b.shape)],
        out_specs=pl.BlockSpec((1, tm, d), row),
        out_shape=jax.ShapeDtypeStruct((bn, seq, d), F32),
        compiler_params=_params(2),
        name="ffn_down_postln",
    )(act, x1, mod_l, w_down, g, b)


def kernel(x, c, w_mod, b_mod, w_in, gmlp_ln_g, gmlp_ln_b, gmlp_w_s, gmlp_b_s, mla_q_norm, mla_w_uq, mla_kv_norm, mla_w_ukv, sconv_w, conf_w_dw, conf_b_dw, conf_ln_g, conf_ln_b, w_out, post_mix_g, post_mix_b, ffn_w_up, ffn_w_conv, ffn_b_conv, ffn_w_down, post_ffn_g, post_ffn_b):
    bn, seq, d = x.shape
    depth = w_mod.shape[0]
    wg = d // N_GROUPS
    heads = wg // MLA_V
    q_lora, kv_lora = mla_w_uq.shape[1], mla_w_ukv.shape[1]
    alpha = (2.0 * depth) ** 0.25
    t = _tiles(seq)
    row2 = lambda a: a.reshape(1, -1)

    mod = _modulation(c, w_mod, b_mod, t["mod_cols"])
    cs, sn = _rope_tables(seq)

    for l in range(depth):
        w_packed = _pack_inproj_weights(w_in[l], wg, q_lora, kv_lora)
        wuq = _pack_uq_weights(mla_w_uq[l], heads)
        b_s_tile = jnp.repeat(gmlp_b_s[l].T, wg // GMLP_HEADS, axis=1)
        yacd, q, k, vt = _inproj(
            x, mod[l], w_packed, cs, sn, row2(gmlp_ln_g[l]), row2(gmlp_ln_b[l]), gmlp_w_s[l],
            b_s_tile, row2(mla_q_norm[l]), wuq, row2(mla_kv_norm[l]), mla_w_ukv[l].astype(BF16),
            sconv_w[l], conf_w_dw[l], row2(conf_b_dw[l]), row2(conf_ln_g[l]), row2(conf_ln_b[l]),
            tm=t["inproj"], wg=wg, heads=heads)
        yb = _attention(q, k, vt, tq=t["attn"], heads=heads)
        w_o = w_out[l].astype(BF16)
        w_acd = jnp.concatenate([w_o[:wg], w_o[2 * wg:]], axis=0)
        x1, h2 = _outproj(yacd, yb, x, mod[l], w_acd, w_o[wg:2 * wg],
                          row2(post_mix_g[l]), row2(post_mix_b[l]), tm=t["outproj"], alpha=alpha)
        act = _ffn_up(h2, ffn_w_up[l].astype(BF16), ffn_w_conv[l], row2(ffn_b_conv[l]),
                      tm=t["ffn_up"], tn=ffn_w_conv.shape[2] // t["ffn_up_col_blocks"],
                      tc=t["ffn_up_chunk"])
        x = _ffn_down(act, x1, mod[l], ffn_w_down[l].astype(BF16), row2(post_ffn_g[l]),
                      row2(post_ffn_b[l]), tm=t["ffn_down"], alpha=alpha)
    return x
```

```python
import functools

import jax
import jax.numpy as jnp
from jax import lax
from jax.experimental import pallas as pl
from jax.experimental.pallas import tpu as pltpu

F32 = jnp.float32
BF16 = jnp.bfloat16

CHUNK = 64
N_GROUPS = 4
GMLP_BLOCK = 128
GMLP_HEADS = 4
MLA_NOPE = 128
MLA_ROPE = 64
MLA_V = 128
ROPE_THETA = 10000.0
SCONV_K = 3
CONF_K = 31
FFN_K = 3
LN_EPS = 1e-5
RMS_EPS = 1e-6

LANES = 128
SUBLANES = 8
MXU_COLS = 256
VMEM_LIMIT_BYTES = 56 * 1024 * 1024

NEG_BIG = -1e30
LOG2_E = 1.4426950408889634


def _tiles(seq):
    def pick(pref):
        t = min(pref, seq)
        assert seq % t == 0
        return t
    return dict(inproj=pick(512), attn=pick(1024), outproj=pick(512),
                ffn_up=pick(512), ffn_up_col_blocks=2, ffn_up_chunk=MXU_COLS,
                ffn_down=pick(512), mod_cols=1024)


def _resident(shape):
    nd = len(shape)
    return pl.BlockSpec(shape, lambda *_: (0,) * nd, pipeline_mode=pl.Buffered(1))


def _params(n_grid):
    return pltpu.CompilerParams(dimension_semantics=("arbitrary",) * n_grid,
                                vmem_limit_bytes=VMEM_LIMIT_BYTES)


def _norm_rows(x):
    mu = jnp.mean(x, axis=-1, keepdims=True)
    xc = x - mu
    var = jnp.mean(xc * xc, axis=-1, keepdims=True)
    return xc * lax.rsqrt(var + LN_EPS)


def _rms_rows(x, gain):
    return x * lax.rsqrt(jnp.mean(x * x, axis=-1, keepdims=True) + RMS_EPS) * gain


def _dot(a, b):
    return jnp.dot(a, b, preferred_element_type=F32)


def _mod_kernel(cb_ref, w_ref, b_ref, o_ref):
    for b in range(cb_ref.shape[0]):
        c = cb_ref[b]
        c_act = c * jax.nn.sigmoid(c)
        for g in range(w_ref.shape[2] // LANES):
            cols = slice(g * LANES, (g + 1) * LANES)
            acc = jnp.sum(w_ref[0, :, cols] * c_act, axis=0, keepdims=True)
            o_ref[0, b:b + 1, cols] = acc + b_ref[0, :, cols]


def _modulation(c, w_mod, b_mod, tn):
    depth, d, n = w_mod.shape
    bn = c.shape[0]
    cb = jnp.broadcast_to(c[:, :, None], (bn, d, LANES))
    out = pl.pallas_call(
        _mod_kernel,
        grid=(depth, n // tn),
        in_specs=[pl.BlockSpec((bn, d, LANES), lambda l, j: (0, 0, 0)),
                  pl.BlockSpec((1, d, tn), lambda l, j: (l, 0, j)),
                  pl.BlockSpec((1, 1, tn), lambda l, j: (l, 0, j))],
        out_specs=pl.BlockSpec((1, bn, tn), lambda l, j: (l, 0, j)),
        out_shape=jax.ShapeDtypeStruct((depth, bn, n), F32),
        compiler_params=_params(2),
        name="modulation",
    )(cb, w_mod, b_mod.reshape(depth, 1, n))
    return out.reshape(depth, bn, 6, d)


def _halo_rows(n_taps):
    return -(-(n_taps - 1) // SUBLANES) * SUBLANES


def _causal_conv(y, tail, w, n_taps):
    halo, tm = tail.shape[0], y.shape[0]
    assert halo % SUBLANES == 0 and halo >= n_taps - 1
    ext = jnp.concatenate([tail, y], axis=0)
    acc = None
    for r in range(min(SUBLANES, n_taps)):
        z = ext if r == 0 else pltpu.roll(ext, r, axis=0)
        for a in range((n_taps - 1 - r) // SUBLANES + 1):
            s = a * SUBLANES + r
            lo = halo - a * SUBLANES
            term = z[lo:lo + tm] * w[n_taps - 1 - s:n_taps - s, :]
            acc = term if acc is None else acc + term
    return acc


def _inproj_kernel(x_ref, mod_ref, w_ref, cs_ref, sn_ref,
                   gln_g_ref, gln_b_ref, ws_ref, bs_ref,
                   qn_g_ref, wuq_ref, kvn_g_ref, wukv_ref,
                   sconv_w_ref, cw_ref, cb_ref, cln_g_ref, cln_b_ref,
                   yacd_ref, q_ref, k_ref, vt_ref,
                   stail_ref, ctail_ref, *, tm, wg, q_lora, kv_lora, heads, scale):
    @pl.when(pl.program_id(1) == 0)
    def _():
        stail_ref[...] = jnp.zeros(stail_ref.shape, F32)
        ctail_ref[...] = jnp.zeros(ctail_ref.shape, F32)

    mod = mod_ref[0]
    h = (_norm_rows(x_ref[0]) * (1.0 + mod[1:2, :]) + mod[0:1, :]).astype(BF16)

    o_uv = 0
    o_q = o_uv + 2 * wg
    o_kv = o_q + q_lora
    o_kr = o_kv + kv_lora
    o_s = o_kr + 2 * LANES
    o_ag = o_s + 3 * wg

    ag = _dot(h, w_ref[:, o_ag:o_ag + 2 * wg])
    glu = ag[:, :wg] * jax.nn.sigmoid(ag[:, wg:])
    y = _causal_conv(glu, ctail_ref[...], cw_ref, CONF_K) + cb_ref[...]
    ctail_ref[...] = glu[tm - ctail_ref.shape[0]:, :]
    y = _norm_rows(y) * cln_g_ref[...] + cln_b_ref[...]
    yacd_ref[0, :, 2 * wg:3 * wg] = (y * jax.nn.sigmoid(y)).astype(BF16)

    s3 = _dot(h, w_ref[:, o_s:o_s + 3 * wg])
    gated = s3[:, wg:2 * wg] * s3[:, 2 * wg:]
    y_c = s3[:, :wg] * _causal_conv(gated, stail_ref[...], sconv_w_ref, SCONV_K)
    stail_ref[...] = gated[tm - stail_ref.shape[0]:, :]
    yacd_ref[0, :, wg:2 * wg] = y_c.astype(BF16)

    z = jax.nn.gelu(_dot(h, w_ref[:, o_uv:o_uv + 2 * wg]), approximate=True)
    u = z[:, :wg]
    v = (_norm_rows(z[:, wg:]) * gln_g_ref[...] + gln_b_ref[...]).astype(BF16)
    nblk = tm // GMLP_BLOCK
    hd = wg // GMLP_HEADS
    ii = lax.broadcasted_iota(jnp.int32, (GMLP_BLOCK, GMLP_BLOCK), 0) // CHUNK
    jj = lax.broadcasted_iota(jnp.int32, (GMLP_BLOCK, GMLP_BLOCK), 1) // CHUNK
    bias = bs_ref[...]
    mixed_cols = []
    for hh in range(GMLP_HEADS):
        w_h = jnp.where(jj <= ii, ws_ref[hh], 0.0).astype(BF16)
        v_h = jnp.concatenate(
            [v[n * GMLP_BLOCK:(n + 1) * GMLP_BLOCK, hh * hd:(hh + 1) * hd] for n in range(nblk)],
            axis=1)
        mixed_cols.append(_dot(w_h, v_h))
    for n in range(nblk):
        mixed = jnp.concatenate([m[:, n * hd:(n + 1) * hd] for m in mixed_cols], axis=1) + bias
        rows = slice(n * GMLP_BLOCK, (n + 1) * GMLP_BLOCK)
        yacd_ref[0, rows, 0:wg] = (u[rows, :] * mixed).astype(BF16)

    cs = cs_ref[...]
    sn = sn_ref[...]
    q_lat = _rms_rows(_dot(h, w_ref[:, o_q:o_q + q_lora]), qn_g_ref[...]).astype(BF16)
    qq = _dot(q_lat, wuq_ref[...])
    kv_lat = _rms_rows(_dot(h, w_ref[:, o_kv:o_kv + kv_lora]), kvn_g_ref[...]).astype(BF16)
    kv = _dot(kv_lat, wukv_ref[...])
    kr2 = _dot(h, w_ref[:, o_kr:o_kr + 2 * LANES])
    k_rope = (kr2[:, :LANES] * cs + kr2[:, LANES:] * sn).astype(BF16)
    hw = 2 * LANES
    for hh in range(heads):
        q_nope = qq[:, hh * hw:hh * hw + LANES]
        q_rope = qq[:, hh * hw + LANES:(hh + 1) * hw]
        q_rot = qq[:, heads * hw + hh * LANES:heads * hw + (hh + 1) * LANES]
        q_ref[0, :, hh * hw:hh * hw + LANES] = (q_nope * scale).astype(BF16)
        q_ref[0, :, hh * hw + LANES:(hh + 1) * hw] = ((q_rope * cs + q_rot * sn) * scale).astype(BF16)
        k_ref[0, :, hh * hw:hh * hw + LANES] = kv[:, hh * hw:hh * hw + LANES].astype(BF16)
        k_ref[0, :, hh * hw + LANES:(hh + 1) * hw] = k_rope
        vt_ref[0, 0, hh * LANES:(hh + 1) * LANES, :] = kv[:, hh * hw + LANES:(hh + 1) * hw].T.astype(BF16)


def _pack_inproj_weights(w_in_l, wg, q_lora, kv_lora):
    d = w_in_l.shape[0]
    o = 0
    w_uv = w_in_l[:, o:o + 2 * wg]; o += 2 * wg
    w_q = w_in_l[:, o:o + q_lora]; o += q_lora
    w_kv = w_in_l[:, o:o + kv_lora]; o += kv_lora
    w_kr = w_in_l[:, o:o + MLA_ROPE]; o += MLA_ROPE
    w_s = w_in_l[:, o:o + 3 * wg]; o += 3 * wg
    w_ag = w_in_l[:, o:o + 2 * wg]
    half = MLA_ROPE // 2
    pad = jnp.zeros((d, LANES - MLA_ROPE), w_in_l.dtype)
    w_kr_rot = jnp.concatenate([-w_kr[:, half:], w_kr[:, :half]], axis=1)
    return jnp.concatenate([w_uv, w_q, w_kv, w_kr, pad, w_kr_rot, pad, w_s, w_ag],
                           axis=1).astype(BF16)


def _pack_uq_weights(w_uq_l, heads):
    r = w_uq_l.shape[0]
    half = MLA_ROPE // 2
    w = w_uq_l.reshape(r, heads, MLA_NOPE + MLA_ROPE)
    pad = jnp.zeros((r, heads, LANES - MLA_ROPE), w.dtype)
    rope = w[:, :, MLA_NOPE:]
    main = jnp.concatenate([w[:, :, :MLA_NOPE], rope, pad], axis=2).reshape(r, heads * 2 * LANES)
    rot = jnp.concatenate([-rope[:, :, half:], rope[:, :, :half], pad], axis=2).reshape(r, heads * LANES)
    return jnp.concatenate([main, rot], axis=1).astype(BF16)


def _rope_tables(seq):
    pos = jnp.arange(seq, dtype=F32)
    inv_freq = ROPE_THETA ** (-jnp.arange(0, MLA_ROPE, 2, dtype=F32) / MLA_ROPE)
    ang = pos[:, None] * inv_freq[None, :]
    zeros = jnp.zeros((seq, LANES - MLA_ROPE), F32)
    cos, sin = jnp.cos(ang), jnp.sin(ang)
    return (jnp.concatenate([cos, cos, zeros], axis=1), jnp.concatenate([sin, sin, zeros], axis=1))


def _inproj(x, mod_l, w_packed, cs, sn, gln_g, gln_b, w_s, b_s_tile, qn_g, wuq, kvn_g, wukv,
            sconv_w, conf_w, conf_b, cln_g, cln_b, *, tm, wg, heads):
    bn, seq, d = x.shape
    q_lora, kv_lora = wuq.shape[0], wukv.shape[0]
    s_halo = _halo_rows(SCONV_K)
    c_halo = _halo_rows(CONF_K)
    kern = functools.partial(
        _inproj_kernel, tm=tm, wg=wg, q_lora=q_lora, kv_lora=kv_lora, heads=heads,
        scale=float((MLA_NOPE + MLA_ROPE) ** -0.5 * LOG2_E))
    row = lambda b, i: (b, i, 0)
    smalls = [gln_g, gln_b, w_s, b_s_tile, qn_g, wuq, kvn_g, wukv, sconv_w, conf_w, conf_b,
              cln_g, cln_b]
    return pl.pallas_call(
        kern,
        grid=(bn, seq // tm),
        in_specs=[pl.BlockSpec((1, tm, d), row),
                  pl.BlockSpec((1, 6, d), lambda b, i: (b, 0, 0)),
                  _resident(w_packed.shape),
                  pl.BlockSpec((tm, LANES), lambda b, i: (i, 0)),
                  pl.BlockSpec((tm, LANES), lambda b, i: (i, 0))]
                 + [_resident(a.shape) for a in smalls],
        out_specs=[pl.BlockSpec((1, tm, 3 * wg), row),
                   pl.BlockSpec((1, tm, heads * 2 * LANES), row),
                   pl.BlockSpec((1, tm, heads * 2 * LANES), row),
                   pl.BlockSpec((1, 1, heads * LANES, tm), lambda b, i: (b, i, 0, 0))],
        out_shape=[jax.ShapeDtypeStruct((bn, seq, 3 * wg), BF16),
                   jax.ShapeDtypeStruct((bn, seq, heads * 2 * LANES), BF16),
                   jax.ShapeDtypeStruct((bn, seq, heads * 2 * LANES), BF16),
                   jax.ShapeDtypeStruct((bn, seq // tm, heads * LANES, tm), BF16)],
        scratch_shapes=[pltpu.VMEM((s_halo, wg), F32), pltpu.VMEM((c_halo, wg), F32)],
        compiler_params=_params(2),
        name="inproj_mixers",
    )(x, mod_l, w_packed, cs, sn, *smalls)


def _attn_kernel(q_ref, k_ref, vt_ref, o_ref, qt_ref, s_ref, mc_ref, m_ref, acc_ref, *, n_sub, tk):
    assert n_sub == 2
    qi = pl.program_id(2)
    first_diag = n_sub * qi
    qt_ref[...] = q_ref[0].astype(F32).T.astype(BF16)
    m_ref[...] = jnp.full(m_ref.shape, NEG_BIG, F32)
    acc_ref[...] = jnp.zeros(acc_ref.shape, F32)
    ones = jnp.ones((acc_ref.shape[1] - MLA_V, tk), BF16)
    all_subs = tuple(range(n_sub))

    def scores(j, slot, subs):
        kb = k_ref[0, pl.ds(pl.multiple_of(j * tk, tk), tk), :]
        for sub in subs:
            st = _dot(kb, qt_ref[:, sub * tk:(sub + 1) * tk])
            s_ref[slot, sub] = st
            mc_ref[slot, sub] = jnp.max(st, axis=0, keepdims=True)

    def update(j, slot, subs, diag_sub=None):
        vtb = jnp.concatenate([vt_ref[0, j], ones], axis=0)
        for sub in subs:
            st = s_ref[slot, sub]
            if sub == diag_sub:
                kc = lax.broadcasted_iota(jnp.int32, (tk, tk), 0) // CHUNK
                qc = lax.broadcasted_iota(jnp.int32, (tk, tk), 1) // CHUNK
                st = jnp.where(kc <= qc, st, NEG_BIG)
                m_cur = jnp.max(st, axis=0, keepdims=True)
            else:
                m_cur = mc_ref[slot, sub]
            m_prev = m_ref[sub]
            m_new = jnp.maximum(m_prev, m_cur)
            p = jnp.exp2(st - m_new).astype(BF16)
            acc_ref[sub] = jnp.exp2(m_prev - m_new) * acc_ref[sub] + _dot(vtb, p)
            m_ref[sub] = m_new

    scores(0, 0, all_subs)

    def body(jj, carry):
        j = n_sub * jj
        scores(j + 1, 1, all_subs)
        update(j, 0, all_subs)
        scores(j + 2, 0, all_subs)
        update(j + 1, 1, all_subs)
        return carry

    lax.fori_loop(0, qi, body, 0)
    scores(first_diag + 1, 1, (1,))
    update(first_diag, 0, all_subs, diag_sub=0)
    update(first_diag + 1, 1, (1,), diag_sub=1)
    for sub in all_subs:
        acc = acc_ref[sub]
        o_ref[0, sub * tk:(sub + 1) * tk, :] = (acc[:MLA_V] / acc[MLA_V:MLA_V + 1]).T.astype(BF16)


def _attention(q, k, vt, *, tq, heads):
    bn, seq, _ = q.shape
    tk = vt.shape[3]
    n_sub = tq // tk
    assert tq == n_sub * tk and n_sub >= 1
    ones_rows = 2 * SUBLANES
    kern = functools.partial(_attn_kernel, n_sub=n_sub, tk=tk)
    return pl.pallas_call(
        kern,
        grid=(bn, heads, seq // tq),
        in_specs=[pl.BlockSpec((1, tq, 2 * LANES), lambda b, h, i: (b, i, h)),
                  pl.BlockSpec((1, seq, 2 * LANES), lambda b, h, i: (b, 0, h)),
                  pl.BlockSpec((1, seq // tk, LANES, tk), lambda b, h, i: (b, 0, h, 0))],
        out_specs=pl.BlockSpec((1, tq, LANES), lambda b, h, i: (b, i, h)),
        out_shape=jax.ShapeDtypeStruct((bn, seq, heads * LANES), BF16),
        scratch_shapes=[pltpu.VMEM((2 * LANES, tq), BF16),
                        pltpu.VMEM((2, n_sub, tk, tk), F32),
                        pltpu.VMEM((2, n_sub, 1, tk), F32),
                        pltpu.VMEM((n_sub, 1, tk), F32),
                        pltpu.VMEM((n_sub, MLA_V + ones_rows, tk), F32)],
        compiler_params=_params(3),
        name="mla_attention",
    )(q, k, vt)


def _outproj_kernel(yacd_ref, yb_ref, x_ref, mod_ref, wacd_ref, wb_ref, g_ref, b_ref,
                    x1_ref, h2_ref, *, alpha, sub_rows):
    mod = mod_ref[0]
    for r in range(x_ref.shape[1] // sub_rows):
        rows = slice(r * sub_rows, (r + 1) * sub_rows)
        y = _dot(yacd_ref[0, rows, :], wacd_ref[...]) + _dot(yb_ref[0, rows, :], wb_ref[...])
        x1 = _norm_rows(alpha * x_ref[0, rows, :] + (1.0 + mod[2:3, :]) * y) * g_ref[...] + b_ref[...]
        x1_ref[0, rows, :] = x1
        h2_ref[0, rows, :] = (_norm_rows(x1) * (1.0 + mod[4:5, :]) + mod[3:4, :]).astype(BF16)


def _outproj(yacd, yb, x, mod_l, w_acd, w_b, g, b, *, tm, alpha):
    bn, seq, d = x.shape
    row = lambda bb, i: (bb, i, 0)
    return pl.pallas_call(
        functools.partial(_outproj_kernel, alpha=alpha, sub_rows=min(tm, MXU_COLS)),
        grid=(bn, seq // tm),
        in_specs=[pl.BlockSpec((1, tm, yacd.shape[2]), row),
                  pl.BlockSpec((1, tm, yb.shape[2]), row),
                  pl.BlockSpec((1, tm, d), row),
                  pl.BlockSpec((1, 6, d), lambda bb, i: (bb, 0, 0)),
                  _resident(w_acd.shape), _resident(w_b.shape),
                  _resident(g.shape), _resident(b.shape)],
        out_specs=[pl.BlockSpec((1, tm, d), row), pl.BlockSpec((1, tm, d), row)],
        out_shape=[jax.ShapeDtypeStruct((bn, seq, d), F32),
                   jax.ShapeDtypeStruct((bn, seq, d), BF16)],
        compiler_params=_params(2),
        name="outproj_postln",
    )(yacd, yb, x, mod_l, w_acd, w_b, g, b)


def _ffn_up_kernel(h_ref, wg_ref, wv_ref, cw_ref, cb_ref, o_ref, tail_ref, *, tm, tc):
    @pl.when(pl.program_id(2) == 0)
    def _():
        tail_ref[...] = jnp.zeros(tail_ref.shape, F32)

    h = h_ref[0]
    halo = tail_ref.shape[0]
    for c in range(wg_ref.shape[1] // tc):
        cols = slice(c * tc, (c + 1) * tc)
        gate = _dot(h, wg_ref[:, cols])
        conv = _causal_conv(gate, tail_ref[:, cols], cw_ref.at[:, cols], FFN_K) + cb_ref[:, cols]
        tail_ref[:, cols] = gate[tm - halo:, :]
        o_ref[0, :, cols] = (conv * jax.nn.sigmoid(conv) * _dot(h, wv_ref[:, cols])).astype(BF16)


def _ffn_up(h2, w_up, conv_w, conv_b, *, tm, tn, tc):
    bn, seq, d = h2.shape
    d_ff = w_up.shape[1] // 2
    nj = d_ff // tn
    assert d_ff == nj * tn and tn % tc == 0
    once = pl.Buffered(1)
    return pl.pallas_call(
        functools.partial(_ffn_up_kernel, tm=tm, tc=tc),
        grid=(nj, bn, seq // tm),
        in_specs=[pl.BlockSpec((1, tm, d), lambda j, b, i: (b, i, 0)),
                  pl.BlockSpec((d, tn), lambda j, b, i: (0, j), pipeline_mode=once),
                  pl.BlockSpec((d, tn), lambda j, b, i: (0, nj + j), pipeline_mode=once),
                  pl.BlockSpec((FFN_K, tn), lambda j, b, i: (0, j)),
                  pl.BlockSpec((1, tn), lambda j, b, i: (0, j))],
        out_specs=pl.BlockSpec((1, tm, tn), lambda j, b, i: (b, i, j)),
        out_shape=jax.ShapeDtypeStruct((bn, seq, d_ff), BF16),
        scratch_shapes=[pltpu.VMEM((_halo_rows(FFN_K), tn), F32)],
        compiler_params=_params(3),
        name="ffn_up_conv",
    )(h2, w_up, w_up, conv_w, conv_b)


def _ffn_down_kernel(a_ref, x_ref, mod_ref, w_ref, g_ref, b_ref, o_ref, *, alpha, sub_rows):
    mod = mod_ref[0]
    for r in range(x_ref.shape[1] // sub_rows):
        rows = slice(r * sub_rows, (r + 1) * sub_rows)
        y = _dot(a_ref[0, rows, :], w_ref[...])
        o_ref[0, rows, :] = (_norm_rows(alpha * x_ref[0, rows, :] + (1.0 + mod[5:6, :]) * y) * g_ref[...]
                             + b_ref[...])


def _ffn_down(act, x1, mod_l, w_down, g, b, *, tm, alpha):
    bn, seq, d = x1.shape
    row = lambda bb, i: (bb, i, 0)
    return pl.pallas_call(
        functools.partial(_ffn_down_kernel, alpha=alpha, sub_rows=min(tm, MXU_COLS)),
        grid=(bn, seq // tm),
        in_specs=[pl.BlockSpec((1, tm, act.shape[2]), row),
                  pl.BlockSpec((1, tm, d), row),
                  pl.BlockSpec((1, 6, d), lambda bb, i: (bb, 0, 0)),
                  _resident(w_down.shape), _resident(g.shape), _resident(b.shape)],
        out_specs=pl.BlockSpec((1, tm, d), row),
        out_shape=jax.ShapeDtypeStruct((bn, seq, d), F32),
        compiler_params=_params(2),
        name="ffn_down_postln",
    )(act, x1, mod_l, w_down, g, b)


def kernel(x, c, w_mod, b_mod, w_in, gmlp_ln_g, gmlp_ln_b, gmlp_w_s, gmlp_b_s, mla_q_norm, mla_w_uq, mla_kv_norm, mla_w_ukv, sconv_w, conf_w_dw, conf_b_dw, conf_ln_g, conf_ln_b, w_out, post_mix_g, post_mix_b, ffn_w_up, ffn_w_conv, ffn_b_conv, ffn_w_down, post_ffn_g, post_ffn_b):
    bn, seq, d = x.shape
    depth = w_mod.shape[0]
    wg = d // N_GROUPS
    heads = wg // MLA_V
    q_lora, kv_lora = mla_w_uq.shape[1], mla_w_ukv.shape[1]
    alpha = (2.0 * depth) ** 0.25
    t = _tiles(seq)
    row2 = lambda a: a.reshape(1, -1)

    mod = _modulation(c, w_mod, b_mod, t["mod_cols"])
    cs, sn = _rope_tables(seq)

    for l in range(depth):
        w_packed = _pack_inproj_weights(w_in[l], wg, q_lora, kv_lora)
        wuq = _pack_uq_weights(mla_w_uq[l], heads)
        b_s_tile = jnp.repeat(gmlp_b_s[l].T, wg // GMLP_HEADS, axis=1)
        yacd, q, k, vt = _inproj(
            x, mod[l], w_packed, cs, sn, row2(gmlp_ln_g[l]), row2(gmlp_ln_b[l]), gmlp_w_s[l],
            b_s_tile, row2(mla_q_norm[l]), wuq, row2(mla_kv_norm[l]), mla_w_ukv[l].astype(BF16),
            sconv_w[l], conf_w_dw[l], row2(conf_b_dw[l]), row2(conf_ln_g[l]), row2(conf_ln_b[l]),
            tm=t["inproj"], wg=wg, heads=heads)
        yb = _attention(q, k, vt, tq=t["attn"], heads=heads)
        w_o = w_out[l].astype(BF16)
        w_acd = jnp.concatenate([w_o[:wg], w_o[2 * wg:]], axis=0)
        x1, h2 = _outproj(yacd, yb, x, mod[l], w_acd, w_o[wg:2 * wg],
                          row2(post_mix_g[l]), row2(post_mix_b[l]), tm=t["outproj"], alpha=alpha)
        act = _ffn_up(h2, ffn_w_up[l].astype(BF16), ffn_w_conv[l], row2(ffn_b_conv[l]),
                      tm=t["ffn_up"], tn=ffn_w_conv.shape[2] // t["ffn_up_col_blocks"],
                      tc=t["ffn_up_chunk"])
        x = _ffn_down(act, x1, mod[l], ffn_w_down[l].astype(BF16), row2(post_ffn_g[l]),
                      row2(post_ffn_b[l]), tm=t["ffn_down"], alpha=alpha)
    return x
```

```python
import functools

import jax
import jax.numpy as jnp
from jax import lax
from jax.experimental import pallas as pl
from jax.experimental.pallas import tpu as pltpu

F32 = jnp.float32
BF16 = jnp.bfloat16

CHUNK = 64
N_GROUPS = 4
GMLP_BLOCK = 128
GMLP_HEADS = 4
MLA_NOPE = 128
MLA_ROPE = 64
MLA_V = 128
ROPE_THETA = 10000.0
SCONV_K = 3
CONF_K = 31
FFN_K = 3
LN_EPS = 1e-5
RMS_EPS = 1e-6

LANES = 128
SUBLANES = 8
MXU_COLS = 256
VMEM_LIMIT_BYTES = 56 * 1024 * 1024

ATTN_PAIRS_PER_ITER = 2
NEG_BIG = -1e30
LOG2_E = 1.4426950408889634


def _tiles(seq):
    def pick(pref):
        t = min(pref, seq)
        assert seq % t == 0
        return t
    return dict(inproj=pick(512), attn=pick(1024), outproj=pick(512),
                ffn_up=pick(512), ffn_up_col_blocks=2, ffn_up_chunk=MXU_COLS,
                ffn_down=pick(512), mod_cols=2048)


def _resident(shape):
    nd = len(shape)
    return pl.BlockSpec(shape, lambda *_: (0,) * nd, pipeline_mode=pl.Buffered(1))


def _params(n_grid):
    return pltpu.CompilerParams(dimension_semantics=("arbitrary",) * n_grid,
                                vmem_limit_bytes=VMEM_LIMIT_BYTES)


def _norm_rows(x):
    mu = jnp.mean(x, axis=-1, keepdims=True)
    xc = x - mu
    var = jnp.mean(xc * xc, axis=-1, keepdims=True)
    return xc * lax.rsqrt(var + LN_EPS)


def _rms_rows(x, gain):
    return x * lax.rsqrt(jnp.mean(x * x, axis=-1, keepdims=True) + RMS_EPS) * gain


def _dot(a, b):
    return jnp.dot(a, b, preferred_element_type=F32)


def _mod_kernel(cb_ref, w_ref, b_ref, o_ref):
    for b in range(cb_ref.shape[0]):
        c = cb_ref[b]
        c_act = c * jax.nn.sigmoid(c)
        for g in range(w_ref.shape[2] // LANES):
            cols = slice(g * LANES, (g + 1) * LANES)
            acc = jnp.sum(w_ref[0, :, cols] * c_act, axis=0, keepdims=True)
            o_ref[0, b:b + 1, cols] = acc + b_ref[0, :, cols]


def _modulation(c, w_mod, b_mod, tn):
    depth, d, n = w_mod.shape
    bn = c.shape[0]
    cb = jnp.broadcast_to(c[:, :, None], (bn, d, LANES))
    out = pl.pallas_call(
        _mod_kernel,
        grid=(depth, n // tn),
        in_specs=[pl.BlockSpec((bn, d, LANES), lambda l, j: (0, 0, 0)),
                  pl.BlockSpec((1, d, tn), lambda l, j: (l, 0, j)),
                  pl.BlockSpec((1, 1, tn), lambda l, j: (l, 0, j))],
        out_specs=pl.BlockSpec((1, bn, tn), lambda l, j: (l, 0, j)),
        out_shape=jax.ShapeDtypeStruct((depth, bn, n), F32),
        compiler_params=_params(2),
        name="modulation",
    )(cb, w_mod, b_mod.reshape(depth, 1, n))
    return out.reshape(depth, bn, 6, d)


def _halo_rows(n_taps):
    return -(-(n_taps - 1) // SUBLANES) * SUBLANES


def _causal_conv(y, tail, w, n_taps):
    halo, tm = tail.shape[0], y.shape[0]
    assert halo % SUBLANES == 0 and halo >= n_taps - 1
    ext = jnp.concatenate([tail, y], axis=0)
    acc = None
    for r in range(min(SUBLANES, n_taps)):
        z = ext if r == 0 else pltpu.roll(ext, r, axis=0)
        for a in range((n_taps - 1 - r) // SUBLANES + 1):
            s = a * SUBLANES + r
            lo = halo - a * SUBLANES
            term = z[lo:lo + tm] * w[n_taps - 1 - s:n_taps - s, :]
            acc = term if acc is None else acc + term
    return acc


def _inproj_kernel(x_ref, mod_ref, w_ref, cs_ref, sn_ref,
                   gln_g_ref, gln_b_ref, ws_ref, bs_ref,
                   qn_g_ref, wuq_ref, kvn_g_ref, wukv_ref,
                   sconv_w_ref, cw_ref, cb_ref, cln_g_ref, cln_b_ref,
                   yacd_ref, q_ref, k_ref, vt_ref,
                   stail_ref, ctail_ref, *, tm, wg, q_lora, kv_lora, heads, scale):
    @pl.when(pl.program_id(1) == 0)
    def _():
        stail_ref[...] = jnp.zeros(stail_ref.shape, F32)
        ctail_ref[...] = jnp.zeros(ctail_ref.shape, F32)

    mod = mod_ref[0]
    h = (_norm_rows(x_ref[0]) * (1.0 + mod[1:2, :]) + mod[0:1, :]).astype(BF16)

    o_uv = 0
    o_q = o_uv + 2 * wg
    o_kv = o_q + q_lora
    o_kr = o_kv + kv_lora
    o_s = o_kr + 2 * LANES
    o_ag = o_s + 3 * wg

    ag = _dot(h, w_ref[:, o_ag:o_ag + 2 * wg])
    glu = ag[:, :wg] * jax.nn.sigmoid(ag[:, wg:])
    y = _causal_conv(glu, ctail_ref[...], cw_ref, CONF_K) + cb_ref[...]
    ctail_ref[...] = glu[tm - ctail_ref.shape[0]:, :]
    y = _norm_rows(y) * cln_g_ref[...] + cln_b_ref[...]
    yacd_ref[0, :, 2 * wg:3 * wg] = (y * jax.nn.sigmoid(y)).astype(BF16)

    s3 = _dot(h, w_ref[:, o_s:o_s + 3 * wg])
    gated = s3[:, wg:2 * wg] * s3[:, 2 * wg:]
    y_c = s3[:, :wg] * _causal_conv(gated, stail_ref[...], sconv_w_ref, SCONV_K)
    stail_ref[...] = gated[tm - stail_ref.shape[0]:, :]
    yacd_ref[0, :, wg:2 * wg] = y_c.astype(BF16)

    z = jax.nn.gelu(_dot(h, w_ref[:, o_uv:o_uv + 2 * wg]), approximate=True)
    u = z[:, :wg]
    v = (_norm_rows(z[:, wg:]) * gln_g_ref[...] + gln_b_ref[...]).astype(BF16)
    nblk = tm // GMLP_BLOCK
    hd = wg // GMLP_HEADS
    ii = lax.broadcasted_iota(jnp.int32, (GMLP_BLOCK, GMLP_BLOCK), 0) // CHUNK
    jj = lax.broadcasted_iota(jnp.int32, (GMLP_BLOCK, GMLP_BLOCK), 1) // CHUNK
    bias = bs_ref[...]
    mixed_cols = []
    for hh in range(GMLP_HEADS):
        w_h = jnp.where(jj <= ii, ws_ref[hh], 0.0).astype(BF16)
        v_h = jnp.concatenate(
            [v[n * GMLP_BLOCK:(n + 1) * GMLP_BLOCK, hh * hd:(hh + 1) * hd] for n in range(nblk)],
            axis=1)
        mixed_cols.append(_dot(w_h, v_h))
    for n in range(nblk):
        mixed = jnp.concatenate([m[:, n * hd:(n + 1) * hd] for m in mixed_cols], axis=1) + bias
        rows = slice(n * GMLP_BLOCK, (n + 1) * GMLP_BLOCK)
        yacd_ref[0, rows, 0:wg] = (u[rows, :] * mixed).astype(BF16)

    cs = cs_ref[...]
    sn = sn_ref[...]
    q_lat = _rms_rows(_dot(h, w_ref[:, o_q:o_q + q_lora]), qn_g_ref[...]).astype(BF16)
    qq = _dot(q_lat, wuq_ref[...])
    kv_lat = _rms_rows(_dot(h, w_ref[:, o_kv:o_kv + kv_lora]), kvn_g_ref[...]).astype(BF16)
    kv = _dot(kv_lat, wukv_ref[...])
    kr2 = _dot(h, w_ref[:, o_kr:o_kr + 2 * LANES])
    k_rope = (kr2[:, :LANES] * cs + kr2[:, LANES:] * sn).astype(BF16)
    hw = 2 * LANES
    for hh in range(heads):
        q_nope = qq[:, hh * hw:hh * hw + LANES]
        q_rope = qq[:, hh * hw + LANES:(hh + 1) * hw]
        q_rot = qq[:, heads * hw + hh * LANES:heads * hw + (hh + 1) * LANES]
        q_ref[0, :, hh * hw:hh * hw + LANES] = (q_nope * scale).astype(BF16)
        q_ref[0, :, hh * hw + LANES:(hh + 1) * hw] = ((q_rope * cs + q_rot * sn) * scale).astype(BF16)
        k_ref[0, :, hh * hw:hh * hw + LANES] = kv[:, hh * hw:hh * hw + LANES].astype(BF16)
        k_ref[0, :, hh * hw + LANES:(hh + 1) * hw] = k_rope
        vt_ref[0, 0, hh * LANES:(hh + 1) * LANES, :] = kv[:, hh * hw + LANES:(hh + 1) * hw].T.astype(BF16)


def _pack_inproj_weights(w_in_l, wg, q_lora, kv_lora):
    d = w_in_l.shape[0]
    o = 0
    w_uv = w_in_l[:, o:o + 2 * wg]; o += 2 * wg
    w_q = w_in_l[:, o:o + q_lora]; o += q_lora
    w_kv = w_in_l[:, o:o + kv_lora]; o += kv_lora
    w_kr = w_in_l[:, o:o + MLA_ROPE]; o += MLA_ROPE
    w_s = w_in_l[:, o:o + 3 * wg]; o += 3 * wg
    w_ag = w_in_l[:, o:o + 2 * wg]
    half = MLA_ROPE // 2
    pad = jnp.zeros((d, LANES - MLA_ROPE), w_in_l.dtype)
    w_kr_rot = jnp.concatenate([-w_kr[:, half:], w_kr[:, :half]], axis=1)
    return jnp.concatenate([w_uv, w_q, w_kv, w_kr, pad, w_kr_rot, pad, w_s, w_ag],
                           axis=1).astype(BF16)


def _pack_uq_weights(w_uq_l, heads):
    r = w_uq_l.shape[0]
    half = MLA_ROPE // 2
    w = w_uq_l.reshape(r, heads, MLA_NOPE + MLA_ROPE)
    pad = jnp.zeros((r, heads, LANES - MLA_ROPE), w.dtype)
    rope = w[:, :, MLA_NOPE:]
    main = jnp.concatenate([w[:, :, :MLA_NOPE], rope, pad], axis=2).reshape(r, heads * 2 * LANES)
    rot = jnp.concatenate([-rope[:, :, half:], rope[:, :, :half], pad], axis=2).reshape(r, heads * LANES)
    return jnp.concatenate([main, rot], axis=1).astype(BF16)


def _rope_tables(seq):
    pos = jnp.arange(seq, dtype=F32)
    inv_freq = ROPE_THETA ** (-jnp.arange(0, MLA_ROPE, 2, dtype=F32) / MLA_ROPE)
    ang = pos[:, None] * inv_freq[None, :]
    zeros = jnp.zeros((seq, LANES - MLA_ROPE), F32)
    cos, sin = jnp.cos(ang), jnp.sin(ang)
    return (jnp.concatenate([cos, cos, zeros], axis=1), jnp.concatenate([sin, sin, zeros], axis=1))


def _inproj(x, mod_l, w_packed, cs, sn, gln_g, gln_b, w_s, b_s_tile, qn_g, wuq, kvn_g, wukv,
            sconv_w, conf_w, conf_b, cln_g, cln_b, *, tm, wg, heads):
    bn, seq, d = x.shape
    q_lora, kv_lora = wuq.shape[0], wukv.shape[0]
    s_halo = _halo_rows(SCONV_K)
    c_halo = _halo_rows(CONF_K)
    kern = functools.partial(
        _inproj_kernel, tm=tm, wg=wg, q_lora=q_lora, kv_lora=kv_lora, heads=heads,
        scale=float((MLA_NOPE + MLA_ROPE) ** -0.5 * LOG2_E))
    row = lambda b, i: (b, i, 0)
    smalls = [gln_g, gln_b, w_s, b_s_tile, qn_g, wuq, kvn_g, wukv, sconv_w, conf_w, conf_b,
              cln_g, cln_b]
    return pl.pallas_call(
        kern,
        grid=(bn, seq // tm),
        in_specs=[pl.BlockSpec((1, tm, d), row),
                  pl.BlockSpec((1, 6, d), lambda b, i: (b, 0, 0)),
                  _resident(w_packed.shape),
                  pl.BlockSpec((tm, LANES), lambda b, i: (i, 0)),
                  pl.BlockSpec((tm, LANES), lambda b, i: (i, 0))]
                 + [_resident(a.shape) for a in smalls],
        out_specs=[pl.BlockSpec((1, tm, 3 * wg), row),
                   pl.BlockSpec((1, tm, heads * 2 * LANES), row),
                   pl.BlockSpec((1, tm, heads * 2 * LANES), row),
                   pl.BlockSpec((1, 1, heads * LANES, tm), lambda b, i: (b, i, 0, 0))],
        out_shape=[jax.ShapeDtypeStruct((bn, seq, 3 * wg), BF16),
                   jax.ShapeDtypeStruct((bn, seq, heads * 2 * LANES), BF16),
                   jax.ShapeDtypeStruct((bn, seq, heads * 2 * LANES), BF16),
                   jax.ShapeDtypeStruct((bn, seq // tm, heads * LANES, tm), BF16)],
        scratch_shapes=[pltpu.VMEM((s_halo, wg), F32), pltpu.VMEM((c_halo, wg), F32)],
        compiler_params=_params(2),
        name="inproj_mixers",
    )(x, mod_l, w_packed, cs, sn, *smalls)


def _attn_kernel(q_ref, k_ref, vt_ref, o_ref, qt_ref, s_ref, mc_ref, m_ref, acc_ref, *, n_sub, tk):
    assert n_sub == 2
    qi = pl.program_id(2)
    first_diag = n_sub * qi
    qt_ref[...] = q_ref[0].astype(F32).T.astype(BF16)
    m_ref[...] = jnp.full(m_ref.shape, NEG_BIG, F32)
    acc_ref[...] = jnp.zeros(acc_ref.shape, F32)
    ones = jnp.ones((acc_ref.shape[1] - MLA_V, tk), BF16)
    all_subs = tuple(range(n_sub))

    def scores(j, slot, subs):
        kb = k_ref[0, pl.ds(pl.multiple_of(j * tk, tk), tk), :]
        for sub in subs:
            st = _dot(kb, qt_ref[:, sub * tk:(sub + 1) * tk])
            s_ref[slot, sub] = st
            mc_ref[slot, sub] = jnp.max(st, axis=0, keepdims=True)

    def update(j, slot, subs, diag_sub=None):
        vtb = jnp.concatenate([vt_ref[0, j], ones], axis=0)
        for sub in subs:
            st = s_ref[slot, sub]
            if sub == diag_sub:
                kc = lax.broadcasted_iota(jnp.int32, (tk, tk), 0) // CHUNK
                qc = lax.broadcasted_iota(jnp.int32, (tk, tk), 1) // CHUNK
                st = jnp.where(kc <= qc, st, NEG_BIG)
                m_cur = jnp.max(st, axis=0, keepdims=True)
            else:
                m_cur = mc_ref[slot, sub]
            m_prev = m_ref[sub]
            m_new = jnp.maximum(m_prev, m_cur)
            p = jnp.exp2(st - m_new).astype(BF16)
            acc_ref[sub] = jnp.exp2(m_prev - m_new) * acc_ref[sub] + _dot(vtb, p)
            m_ref[sub] = m_new

    scores(0, 0, all_subs)

    def tile_pair(j):
        scores(j + 1, 1, all_subs)
        update(j, 0, all_subs)
        scores(j + 2, 0, all_subs)
        update(j + 1, 1, all_subs)

    def body(jj, carry):
        for p in range(ATTN_PAIRS_PER_ITER):
            tile_pair(2 * (ATTN_PAIRS_PER_ITER * jj + p))
        return carry

    lax.fori_loop(0, qi // ATTN_PAIRS_PER_ITER, body, 0)
    for left in range(1, ATTN_PAIRS_PER_ITER):
        @pl.when(qi % ATTN_PAIRS_PER_ITER >= left)
        def _():
            tile_pair(2 * (qi - qi % ATTN_PAIRS_PER_ITER + left - 1))
    scores(first_diag + 1, 1, (1,))
    update(first_diag, 0, all_subs, diag_sub=0)
    update(first_diag + 1, 1, (1,), diag_sub=1)
    for sub in all_subs:
        acc = acc_ref[sub]
        o_ref[0, sub * tk:(sub + 1) * tk, :] = (acc[:MLA_V] / acc[MLA_V:MLA_V + 1]).T.astype(BF16)


def _attention(q, k, vt, *, tq, heads):
    bn, seq, _ = q.shape
    tk = vt.shape[3]
    n_sub = tq // tk
    assert tq == n_sub * tk and n_sub >= 1
    ones_rows = 2 * SUBLANES
    kern = functools.partial(_attn_kernel, n_sub=n_sub, tk=tk)
    return pl.pallas_call(
        kern,
        grid=(bn, heads, seq // tq),
        in_specs=[pl.BlockSpec((1, tq, 2 * LANES), lambda b, h, i: (b, i, h)),
                  pl.BlockSpec((1, seq, 2 * LANES), lambda b, h, i: (b, 0, h)),
                  pl.BlockSpec((1, seq // tk, LANES, tk), lambda b, h, i: (b, 0, h, 0))],
        out_specs=pl.BlockSpec((1, tq, LANES), lambda b, h, i: (b, i, h)),
        out_shape=jax.ShapeDtypeStruct((bn, seq, heads * LANES), BF16),
        scratch_shapes=[pltpu.VMEM((2 * LANES, tq), BF16),
                        pltpu.VMEM((2, n_sub, tk, tk), F32),
                        pltpu.VMEM((2, n_sub, 1, tk), F32),
                        pltpu.VMEM((n_sub, 1, tk), F32),
                        pltpu.VMEM((n_sub, MLA_V + ones_rows, tk), F32)],
        compiler_params=_params(3),
        name="mla_attention",
    )(q, k, vt)


def _outproj_kernel(yacd_ref, yb_ref, x_ref, mod_ref, wacd_ref, wb_ref, g_ref, b_ref,
                    x1_ref, h2_ref, *, alpha, sub_rows):
    mod = mod_ref[0]
    for r in range(x_ref.shape[1] // sub_rows):
        rows = slice(r * sub_rows, (r + 1) * sub_rows)
        y = _dot(yacd_ref[0, rows, :], wacd_ref[...]) + _dot(yb_ref[0, rows, :], wb_ref[...])
        x1 = _norm_rows(alpha * x_ref[0, rows, :] + (1.0 + mod[2:3, :]) * y) * g_ref[...] + b_ref[...]
        x1_ref[0, rows, :] = x1
        h2_ref[0, rows, :] = (_norm_rows(x1) * (1.0 + mod[4:5, :]) + mod[3:4, :]).astype(BF16)


def _outproj(yacd, yb, x, mod_l, w_acd, w_b, g, b, *, tm, alpha):
    bn, seq, d = x.shape
    row = lambda bb, i: (bb, i, 0)
    return pl.pallas_call(
        functools.partial(_outproj_kernel, alpha=alpha, sub_rows=min(tm, MXU_COLS)),
        grid=(bn, seq // tm),
        in_specs=[pl.BlockSpec((1, tm, yacd.shape[2]), row),
                  pl.BlockSpec((1, tm, yb.shape[2]), row),
                  pl.BlockSpec((1, tm, d), row),
                  pl.BlockSpec((1, 6, d), lambda bb, i: (bb, 0, 0)),
                  _resident(w_acd.shape), _resident(w_b.shape),
                  _resident(g.shape), _resident(b.shape)],
        out_specs=[pl.BlockSpec((1, tm, d), row), pl.BlockSpec((1, tm, d), row)],
        out_shape=[jax.ShapeDtypeStruct((bn, seq, d), F32),
                   jax.ShapeDtypeStruct((bn, seq, d), BF16)],
        compiler_params=_params(2),
        name="outproj_postln",
    )(yacd, yb, x, mod_l, w_acd, w_b, g, b)


def _ffn_up_kernel(h_ref, wg_ref, wv_ref, cw_ref, cb_ref, o_ref, tail_ref, *, tm, tc):
    @pl.when(pl.program_id(2) == 0)
    def _():
        tail_ref[...] = jnp.zeros(tail_ref.shape, F32)

    h = h_ref[0]
    halo = tail_ref.shape[0]
    for c in range(wg_ref.shape[1] // tc):
        cols = slice(c * tc, (c + 1) * tc)
        gate = _dot(h, wg_ref[:, cols])
        conv = _causal_conv(gate, tail_ref[:, cols], cw_ref.at[:, cols], FFN_K) + cb_ref[:, cols]
        tail_ref[:, cols] = gate[tm - halo:, :]
        o_ref[0, :, cols] = (conv * jax.nn.sigmoid(conv) * _dot(h, wv_ref[:, cols])).astype(BF16)


def _ffn_up(h2, w_up, conv_w, conv_b, *, tm, tn, tc):
    bn, seq, d = h2.shape
    d_ff = w_up.shape[1] // 2
    nj = d_ff // tn
    assert d_ff == nj * tn and tn % tc == 0
    once = pl.Buffered(1)
    return pl.pallas_call(
        functools.partial(_ffn_up_kernel, tm=tm, tc=tc),
        grid=(nj, bn, seq // tm),
        in_specs=[pl.BlockSpec((1, tm, d), lambda j, b, i: (b, i, 0)),
                  pl.BlockSpec((d, tn), lambda j, b, i: (0, j), pipeline_mode=once),
                  pl.BlockSpec((d, tn), lambda j, b, i: (0, nj + j), pipeline_mode=once),
                  pl.BlockSpec((FFN_K, tn), lambda j, b, i: (0, j)),
                  pl.BlockSpec((1, tn), lambda j, b, i: (0, j))],
        out_specs=pl.BlockSpec((1, tm, tn), lambda j, b, i: (b, i, j)),
        out_shape=jax.ShapeDtypeStruct((bn, seq, d_ff), BF16),
        scratch_shapes=[pltpu.VMEM((_halo_rows(FFN_K), tn), F32)],
        compiler_params=_params(3),
        name="ffn_up_conv",
    )(h2, w_up, w_up, conv_w, conv_b)


def _ffn_down_kernel(a_ref, x_ref, mod_ref, w_ref, g_ref, b_ref, o_ref, *, alpha, sub_rows):
    mod = mod_ref[0]
    for r in range(x_ref.shape[1] // sub_rows):
        rows = slice(r * sub_rows, (r + 1) * sub_rows)
        y = _dot(a_ref[0, rows, :], w_ref[...])
        o_ref[0, rows, :] = (_norm_rows(alpha * x_ref[0, rows, :] + (1.0 + mod[5:6, :]) * y) * g_ref[...]
                             + b_ref[...])


def _ffn_down(act, x1, mod_l, w_down, g, b, *, tm, alpha):
    bn, seq, d = x1.shape
    row = lambda bb, i: (bb, i, 0)
    return pl.pallas_call(
        functools.partial(_ffn_down_kernel, alpha=alpha, sub_rows=min(tm, MXU_COLS)),
        grid=(bn, seq // tm),
        in_specs=[pl.BlockSpec((1, tm, act.shape[2]), row),
                  pl.BlockSpec((1, tm, d), row),
                  pl.BlockSpec((1, 6, d), lambda bb, i: (bb, 0, 0)),
                  _resident(w_down.shape), _resident(g.shape), _resident(b.shape)],
        out_specs=pl.BlockSpec((1, tm, d), row),
        out_shape=jax.ShapeDtypeStruct((bn, seq, d), F32),
        compiler_params=_params(2),
        name="ffn_down_postln",
    )(act, x1, mod_l, w_down, g, b)


def kernel(x, c, w_mod, b_mod, w_in, gmlp_ln_g, gmlp_ln_b, gmlp_w_s, gmlp_b_s, mla_q_norm, mla_w_uq, mla_kv_norm, mla_w_ukv, sconv_w, conf_w_dw, conf_b_dw, conf_ln_g, conf_ln_b, w_out, post_mix_g, post_mix_b, ffn_w_up, ffn_w_conv, ffn_b_conv, ffn_w_down, post_ffn_g, post_ffn_b):
    bn, seq, d = x.shape
    depth = w_mod.shape[0]
    wg = d // N_GROUPS
    heads = wg // MLA_V
    q_lora, kv_lora = mla_w_uq.shape[1], mla_w_ukv.shape[1]
    alpha = (2.0 * depth) ** 0.25
    t = _tiles(seq)
    row2 = lambda a: a.reshape(1, -1)

    mod = _modulation(c, w_mod, b_mod, t["mod_cols"])
    cs, sn = _rope_tables(seq)

    for l in range(depth):
        w_packed = _pack_inproj_weights(w_in[l], wg, q_lora, kv_lora)
        wuq = _pack_uq_weights(mla_w_uq[l], heads)
        b_s_tile = jnp.repeat(gmlp_b_s[l].T, wg // GMLP_HEADS, axis=1)
        yacd, q, k, vt = _inproj(
            x, mod[l], w_packed, cs, sn, row2(gmlp_ln_g[l]), row2(gmlp_ln_b[l]), gmlp_w_s[l],
            b_s_tile, row2(mla_q_norm[l]), wuq, row2(mla_kv_norm[l]), mla_w_ukv[l].astype(BF16),
            sconv_w[l], conf_w_dw[l], row2(conf_b_dw[l]), row2(conf_ln_g[l]), row2(conf_ln_b[l]),
            tm=t["inproj"], wg=wg, heads=heads)
        yb = _attention(q, k, vt, tq=t["attn"], heads=heads)
        w_o = w_out[l].astype(BF16)
        w_acd = jnp.concatenate([w_o[:wg], w_o[2 * wg:]], axis=0)
        x1, h2 = _outproj(yacd, yb, x, mod[l], w_acd, w_o[wg:2 * wg],
                          row2(post_mix_g[l]), row2(post_mix_b[l]), tm=t["outproj"], alpha=alpha)
        act = _ffn_up(h2, ffn_w_up[l].astype(BF16), ffn_w_conv[l], row2(ffn_b_conv[l]),
                      tm=t["ffn_up"], tn=ffn_w_conv.shape[2] // t["ffn_up_col_blocks"],
                      tc=t["ffn_up_chunk"])
        x = _ffn_down(act, x1, mod[l], ffn_w_down[l].astype(BF16), row2(post_ffn_g[l]),
                      row2(post_ffn_b[l]), tm=t["ffn_down"], alpha=alpha)
    return x
```

```python
import functools

import jax
import jax.numpy as jnp
from jax import lax
from jax.experimental import pallas as pl
from jax.experimental.pallas import tpu as pltpu

F32 = jnp.float32
BF16 = jnp.bfloat16

CHUNK = 64
N_GROUPS = 4
GMLP_BLOCK = 128
GMLP_HEADS = 4
MLA_NOPE = 128
MLA_ROPE = 64
MLA_V = 128
ROPE_THETA = 10000.0
SCONV_K = 3
CONF_K = 31
FFN_K = 3
LN_EPS = 1e-5
RMS_EPS = 1e-6

LANES = 128
SUBLANES = 8
MXU_COLS = 256
VMEM_LIMIT_BYTES = 56 * 1024 * 1024

ATTN_PAIRS_PER_ITER = 2
NEG_BIG = -1e30
LOG2_E = 1.4426950408889634


def _tiles(seq):
    def pick(pref):
        t = min(pref, seq)
        assert seq % t == 0
        return t
    return dict(inproj=pick(512), attn=pick(1024), outproj=pick(512),
                ffn_up=pick(512), ffn_up_col_blocks=2, ffn_up_chunk=MXU_COLS,
                ffn_down=pick(512), mod_cols=2048)


def _resident(shape):
    nd = len(shape)
    return pl.BlockSpec(shape, lambda *_: (0,) * nd, pipeline_mode=pl.Buffered(1))


def _layer_resident(stacked_shape, layer):
    nd = len(stacked_shape) - 1
    return pl.BlockSpec((None,) + tuple(stacked_shape[1:]), lambda *_: (layer,) + (0,) * nd,
                        pipeline_mode=pl.Buffered(1))


def _params(n_grid):
    return pltpu.CompilerParams(dimension_semantics=("arbitrary",) * n_grid,
                                vmem_limit_bytes=VMEM_LIMIT_BYTES)


def _norm_rows(x):
    mu = jnp.mean(x, axis=-1, keepdims=True)
    xc = x - mu
    var = jnp.mean(xc * xc, axis=-1, keepdims=True)
    return xc * lax.rsqrt(var + LN_EPS)


def _rms_rows(x, gain):
    return x * lax.rsqrt(jnp.mean(x * x, axis=-1, keepdims=True) + RMS_EPS) * gain


def _dot(a, b):
    return jnp.dot(a, b, preferred_element_type=F32)


def _mod_kernel(cb_ref, w_ref, b_ref, o_ref):
    for b in range(cb_ref.shape[0]):
        c = cb_ref[b]
        c_act = c * jax.nn.sigmoid(c)
        for g in range(w_ref.shape[2] // LANES):
            cols = slice(g * LANES, (g + 1) * LANES)
            acc = jnp.sum(w_ref[0, :, cols] * c_act, axis=0, keepdims=True)
            o_ref[0, b:b + 1, cols] = acc + b_ref[0, :, cols]


def _modulation(c, w_mod, b_mod, tn):
    depth, d, n = w_mod.shape
    bn = c.shape[0]
    cb = jnp.broadcast_to(c[:, :, None], (bn, d, LANES))
    out = pl.pallas_call(
        _mod_kernel,
        grid=(depth, n // tn),
        in_specs=[pl.BlockSpec((bn, d, LANES), lambda l, j: (0, 0, 0)),
                  pl.BlockSpec((1, d, tn), lambda l, j: (l, 0, j)),
                  pl.BlockSpec((1, 1, tn), lambda l, j: (l, 0, j))],
        out_specs=pl.BlockSpec((1, bn, tn), lambda l, j: (l, 0, j)),
        out_shape=jax.ShapeDtypeStruct((depth, bn, n), F32),
        compiler_params=_params(2),
        name="modulation",
    )(cb, w_mod, b_mod.reshape(depth, 1, n))
    return out.reshape(depth, bn, 6, d)


def _halo_rows(n_taps):
    return -(-(n_taps - 1) // SUBLANES) * SUBLANES


def _causal_conv(y, tail, w, n_taps):
    halo, tm = tail.shape[0], y.shape[0]
    assert halo % SUBLANES == 0 and halo >= n_taps - 1
    ext = jnp.concatenate([tail, y], axis=0)
    acc = None
    for r in range(min(SUBLANES, n_taps)):
        z = ext if r == 0 else pltpu.roll(ext, r, axis=0)
        for a in range((n_taps - 1 - r) // SUBLANES + 1):
            s = a * SUBLANES + r
            lo = halo - a * SUBLANES
            term = z[lo:lo + tm] * w[n_taps - 1 - s:n_taps - s, :]
            acc = term if acc is None else acc + term
    return acc


def _inproj_kernel(x_ref, mod_ref, w_ref, cs_ref, sn_ref,
                   gln_g_ref, gln_b_ref, ws_ref, bs_ref,
                   qn_g_ref, wuq_ref, kvn_g_ref, wukv_ref,
                   sconv_w_ref, cw_ref, cb_ref, cln_g_ref, cln_b_ref,
                   yacd_ref, q_ref, k_ref, vt_ref,
                   stail_ref, ctail_ref, *, tm, wg, q_lora, kv_lora, heads, scale):
    @pl.when(pl.program_id(1) == 0)
    def _():
        stail_ref[...] = jnp.zeros(stail_ref.shape, F32)
        ctail_ref[...] = jnp.zeros(ctail_ref.shape, F32)

    mod = mod_ref[0]
    h = (_norm_rows(x_ref[0]) * (1.0 + mod[1:2, :]) + mod[0:1, :]).astype(BF16)

    o_uv = 0
    o_q = o_uv + 2 * wg
    o_kv = o_q + q_lora
    o_kr = o_kv + kv_lora
    o_s = o_kr + 2 * LANES
    o_ag = o_s + 3 * wg

    ag = _dot(h, w_ref[:, o_ag:o_ag + 2 * wg])
    glu = ag[:, :wg] * jax.nn.sigmoid(ag[:, wg:])
    y = _causal_conv(glu, ctail_ref[...], cw_ref, CONF_K) + cb_ref[...]
    ctail_ref[...] = glu[tm - ctail_ref.shape[0]:, :]
    y = _norm_rows(y) * cln_g_ref[...] + cln_b_ref[...]
    yacd_ref[0, :, 2 * wg:3 * wg] = (y * jax.nn.sigmoid(y)).astype(BF16)

    s3 = _dot(h, w_ref[:, o_s:o_s + 3 * wg])
    gated = s3[:, wg:2 * wg] * s3[:, 2 * wg:]
    y_c = s3[:, :wg] * _causal_conv(gated, stail_ref[...], sconv_w_ref, SCONV_K)
    stail_ref[...] = gated[tm - stail_ref.shape[0]:, :]
    yacd_ref[0, :, wg:2 * wg] = y_c.astype(BF16)

    z = jax.nn.gelu(_dot(h, w_ref[:, o_uv:o_uv + 2 * wg]), approximate=True)
    u = z[:, :wg]
    v = (_norm_rows(z[:, wg:]) * gln_g_ref[...] + gln_b_ref[...]).astype(BF16)
    nblk = tm // GMLP_BLOCK
    hd = wg // GMLP_HEADS
    ii = lax.broadcasted_iota(jnp.int32, (GMLP_BLOCK, GMLP_BLOCK), 0) // CHUNK
    jj = lax.broadcasted_iota(jnp.int32, (GMLP_BLOCK, GMLP_BLOCK), 1) // CHUNK
    bias = bs_ref[...]
    mixed_cols = []
    for hh in range(GMLP_HEADS):
        w_h = jnp.where(jj <= ii, ws_ref[hh], 0.0).astype(BF16)
        v_h = jnp.concatenate(
            [v[n * GMLP_BLOCK:(n + 1) * GMLP_BLOCK, hh * hd:(hh + 1) * hd] for n in range(nblk)],
            axis=1)
        mixed_cols.append(_dot(w_h, v_h))
    for n in range(nblk):
        mixed = jnp.concatenate([m[:, n * hd:(n + 1) * hd] for m in mixed_cols], axis=1) + bias
        rows = slice(n * GMLP_BLOCK, (n + 1) * GMLP_BLOCK)
        yacd_ref[0, rows, 0:wg] = (u[rows, :] * mixed).astype(BF16)

    cs = cs_ref[...]
    sn = sn_ref[...]
    q_lat = _rms_rows(_dot(h, w_ref[:, o_q:o_q + q_lora]), qn_g_ref[...]).astype(BF16)
    qq = _dot(q_lat, wuq_ref[...])
    kv_lat = _rms_rows(_dot(h, w_ref[:, o_kv:o_kv + kv_lora]), kvn_g_ref[...]).astype(BF16)
    kv = _dot(kv_lat, wukv_ref[...])
    kr2 = _dot(h, w_ref[:, o_kr:o_kr + 2 * LANES])
    k_rope = (kr2[:, :LANES] * cs + kr2[:, LANES:] * sn).astype(BF16)
    hw = 2 * LANES
    for hh in range(heads):
        q_nope = qq[:, hh * hw:hh * hw + LANES]
        q_rope = qq[:, hh * hw + LANES:(hh + 1) * hw]
        q_rot = qq[:, heads * hw + hh * LANES:heads * hw + (hh + 1) * LANES]
        q_ref[0, :, hh * hw:hh * hw + LANES] = (q_nope * scale).astype(BF16)
        q_ref[0, :, hh * hw + LANES:(hh + 1) * hw] = ((q_rope * cs + q_rot * sn) * scale).astype(BF16)
        k_ref[0, :, hh * hw:hh * hw + LANES] = kv[:, hh * hw:hh * hw + LANES].astype(BF16)
        k_ref[0, :, hh * hw + LANES:(hh + 1) * hw] = k_rope
        vt_ref[0, 0, hh * LANES:(hh + 1) * LANES, :] = kv[:, hh * hw + LANES:(hh + 1) * hw].T.astype(BF16)


def _pack_inproj_weights(w_in_l, wg, q_lora, kv_lora):
    d = w_in_l.shape[0]
    o = 0
    w_uv = w_in_l[:, o:o + 2 * wg]; o += 2 * wg
    w_q = w_in_l[:, o:o + q_lora]; o += q_lora
    w_kv = w_in_l[:, o:o + kv_lora]; o += kv_lora
    w_kr = w_in_l[:, o:o + MLA_ROPE]; o += MLA_ROPE
    w_s = w_in_l[:, o:o + 3 * wg]; o += 3 * wg
    w_ag = w_in_l[:, o:o + 2 * wg]
    half = MLA_ROPE // 2
    pad = jnp.zeros((d, LANES - MLA_ROPE), w_in_l.dtype)
    w_kr_rot = jnp.concatenate([-w_kr[:, half:], w_kr[:, :half]], axis=1)
    return jnp.concatenate([w_uv, w_q, w_kv, w_kr, pad, w_kr_rot, pad, w_s, w_ag],
                           axis=1).astype(BF16)


def _pack_uq_weights(w_uq_l, heads):
    r = w_uq_l.shape[0]
    half = MLA_ROPE // 2
    w = w_uq_l.reshape(r, heads, MLA_NOPE + MLA_ROPE)
    pad = jnp.zeros((r, heads, LANES - MLA_ROPE), w.dtype)
    rope = w[:, :, MLA_NOPE:]
    main = jnp.concatenate([w[:, :, :MLA_NOPE], rope, pad], axis=2).reshape(r, heads * 2 * LANES)
    rot = jnp.concatenate([-rope[:, :, half:], rope[:, :, :half], pad], axis=2).reshape(r, heads * LANES)
    return jnp.concatenate([main, rot], axis=1).astype(BF16)


def _rope_tables(seq):
    pos = jnp.arange(seq, dtype=F32)
    inv_freq = ROPE_THETA ** (-jnp.arange(0, MLA_ROPE, 2, dtype=F32) / MLA_ROPE)
    ang = pos[:, None] * inv_freq[None, :]
    zeros = jnp.zeros((seq, LANES - MLA_ROPE), F32)
    cos, sin = jnp.cos(ang), jnp.sin(ang)
    return (jnp.concatenate([cos, cos, zeros], axis=1), jnp.concatenate([sin, sin, zeros], axis=1))


def _inproj(x, mod_l, w_packed, cs, sn, gln_g, gln_b, w_s, b_s_tile, qn_g, wuq, kvn_g, wukv,
            sconv_w, conf_w, conf_b, cln_g, cln_b, *, tm, wg, heads):
    bn, seq, d = x.shape
    q_lora, kv_lora = wuq.shape[0], wukv.shape[0]
    s_halo = _halo_rows(SCONV_K)
    c_halo = _halo_rows(CONF_K)
    kern = functools.partial(
        _inproj_kernel, tm=tm, wg=wg, q_lora=q_lora, kv_lora=kv_lora, heads=heads,
        scale=float((MLA_NOPE + MLA_ROPE) ** -0.5 * LOG2_E))
    row = lambda b, i: (b, i, 0)
    smalls = [gln_g, gln_b, w_s, b_s_tile, qn_g, wuq, kvn_g, wukv, sconv_w, conf_w, conf_b,
              cln_g, cln_b]
    return pl.pallas_call(
        kern,
        grid=(bn, seq // tm),
        in_specs=[pl.BlockSpec((1, tm, d), row),
                  pl.BlockSpec((1, 6, d), lambda b, i: (b, 0, 0)),
                  _resident(w_packed.shape),
                  pl.BlockSpec((tm, LANES), lambda b, i: (i, 0)),
                  pl.BlockSpec((tm, LANES), lambda b, i: (i, 0))]
                 + [_resident(a.shape) for a in smalls],
        out_specs=[pl.BlockSpec((1, tm, 3 * wg), row),
                   pl.BlockSpec((1, tm, heads * 2 * LANES), row),
                   pl.BlockSpec((1, tm, heads * 2 * LANES), row),
                   pl.BlockSpec((1, 1, heads * LANES, tm), lambda b, i: (b, i, 0, 0))],
        out_shape=[jax.ShapeDtypeStruct((bn, seq, 3 * wg), BF16),
                   jax.ShapeDtypeStruct((bn, seq, heads * 2 * LANES), BF16),
                   jax.ShapeDtypeStruct((bn, seq, heads * 2 * LANES), BF16),
                   jax.ShapeDtypeStruct((bn, seq // tm, heads * LANES, tm), BF16)],
        scratch_shapes=[pltpu.VMEM((s_halo, wg), F32), pltpu.VMEM((c_halo, wg), F32)],
        compiler_params=_params(2),
        name="inproj_mixers",
    )(x, mod_l, w_packed, cs, sn, *smalls)


def _attn_kernel(q_ref, k_ref, vt_ref, o_ref, qt_ref, s_ref, mc_ref, m_ref, acc_ref, *, n_sub, tk):
    assert n_sub == 2
    qi = pl.program_id(2)
    first_diag = n_sub * qi
    qt_ref[...] = q_ref[0].astype(F32).T.astype(BF16)
    m_ref[...] = jnp.full(m_ref.shape, NEG_BIG, F32)
    acc_ref[...] = jnp.zeros(acc_ref.shape, F32)
    ones = jnp.ones((acc_ref.shape[1] - MLA_V, tk), BF16)
    all_subs = tuple(range(n_sub))

    def scores(j, slot, subs):
        kb = k_ref[0, pl.ds(pl.multiple_of(j * tk, tk), tk), :]
        for sub in subs:
            st = _dot(kb, qt_ref[:, sub * tk:(sub + 1) * tk])
            s_ref[slot, sub] = st
            mc_ref[slot, sub] = jnp.max(st, axis=0, keepdims=True)

    def update(j, slot, subs, diag_sub=None):
        vtb = jnp.concatenate([vt_ref[0, j], ones], axis=0)
        for sub in subs:
            st = s_ref[slot, sub]
            if sub == diag_sub:
                kc = lax.broadcasted_iota(jnp.int32, (tk, tk), 0) // CHUNK
                qc = lax.broadcasted_iota(jnp.int32, (tk, tk), 1) // CHUNK
                st = jnp.where(kc <= qc, st, NEG_BIG)
                m_cur = jnp.max(st, axis=0, keepdims=True)
            else:
                m_cur = mc_ref[slot, sub]
            m_prev = m_ref[sub]
            m_new = jnp.maximum(m_prev, m_cur)
            p = jnp.exp2(st - m_new).astype(BF16)
            acc_ref[sub] = jnp.exp2(m_prev - m_new) * acc_ref[sub] + _dot(vtb, p)
            m_ref[sub] = m_new

    scores(0, 0, all_subs)

    def tile_pair(j):
        scores(j + 1, 1, all_subs)
        update(j, 0, all_subs)
        scores(j + 2, 0, all_subs)
        update(j + 1, 1, all_subs)

    def body(jj, carry):
        for p in range(ATTN_PAIRS_PER_ITER):
            tile_pair(2 * (ATTN_PAIRS_PER_ITER * jj + p))
        return carry

    lax.fori_loop(0, qi // ATTN_PAIRS_PER_ITER, body, 0)
    for left in range(1, ATTN_PAIRS_PER_ITER):
        @pl.when(qi % ATTN_PAIRS_PER_ITER >= left)
        def _():
            tile_pair(2 * (qi - qi % ATTN_PAIRS_PER_ITER + left - 1))
    scores(first_diag + 1, 1, (1,))
    update(first_diag, 0, all_subs, diag_sub=0)
    update(first_diag + 1, 1, (1,), diag_sub=1)
    for sub in all_subs:
        acc = acc_ref[sub]
        o_ref[0, sub * tk:(sub + 1) * tk, :] = (acc[:MLA_V] / acc[MLA_V:MLA_V + 1]).T.astype(BF16)


def _attention(q, k, vt, *, tq, heads):
    bn, seq, _ = q.shape
    tk = vt.shape[3]
    n_sub = tq // tk
    assert tq == n_sub * tk and n_sub >= 1
    ones_rows = 2 * SUBLANES
    kern = functools.partial(_attn_kernel, n_sub=n_sub, tk=tk)
    return pl.pallas_call(
        kern,
        grid=(bn, heads, seq // tq),
        in_specs=[pl.BlockSpec((1, tq, 2 * LANES), lambda b, h, i: (b, i, h)),
                  pl.BlockSpec((1, seq, 2 * LANES), lambda b, h, i: (b, 0, h)),
                  pl.BlockSpec((1, seq // tk, LANES, tk), lambda b, h, i: (b, 0, h, 0))],
        out_specs=pl.BlockSpec((1, tq, LANES), lambda b, h, i: (b, i, h)),
        out_shape=jax.ShapeDtypeStruct((bn, seq, heads * LANES), BF16),
        scratch_shapes=[pltpu.VMEM((2 * LANES, tq), BF16),
                        pltpu.VMEM((2, n_sub, tk, tk), F32),
                        pltpu.VMEM((2, n_sub, 1, tk), F32),
                        pltpu.VMEM((n_sub, 1, tk), F32),
                        pltpu.VMEM((n_sub, MLA_V + ones_rows, tk), F32)],
        compiler_params=_params(3),
        name="mla_attention",
    )(q, k, vt)


def _outproj_kernel(yacd_ref, yb_ref, x_ref, mod_ref, w_ref, g_ref, b_ref,
                    x1_ref, h2_ref, *, alpha, sub_rows):
    mod = mod_ref[0]
    wg = yb_ref.shape[2]
    for r in range(x_ref.shape[1] // sub_rows):
        rows = slice(r * sub_rows, (r + 1) * sub_rows)
        y = (_dot(yacd_ref[0, rows, :wg], w_ref[:wg, :]) + _dot(yb_ref[0, rows, :], w_ref[wg:2 * wg, :])
             + _dot(yacd_ref[0, rows, wg:], w_ref[2 * wg:, :]))
        x1 = _norm_rows(alpha * x_ref[0, rows, :] + (1.0 + mod[2:3, :]) * y) * g_ref[...] + b_ref[...]
        x1_ref[0, rows, :] = x1
        h2_ref[0, rows, :] = (_norm_rows(x1) * (1.0 + mod[4:5, :]) + mod[3:4, :]).astype(BF16)


def _outproj(yacd, yb, x, mod_l, w_all, layer, g, b, *, tm, alpha):
    bn, seq, d = x.shape
    row = lambda bb, i: (bb, i, 0)
    return pl.pallas_call(
        functools.partial(_outproj_kernel, alpha=alpha, sub_rows=min(tm, MXU_COLS)),
        grid=(bn, seq // tm),
        in_specs=[pl.BlockSpec((1, tm, yacd.shape[2]), row),
                  pl.BlockSpec((1, tm, yb.shape[2]), row),
                  pl.BlockSpec((1, tm, d), row),
                  pl.BlockSpec((1, 6, d), lambda bb, i: (bb, 0, 0)),
                  _layer_resident(w_all.shape, layer),
                  _resident(g.shape), _resident(b.shape)],
        out_specs=[pl.BlockSpec((1, tm, d), row), pl.BlockSpec((1, tm, d), row)],
        out_shape=[jax.ShapeDtypeStruct((bn, seq, d), F32),
                   jax.ShapeDtypeStruct((bn, seq, d), BF16)],
        compiler_params=_params(2),
        name="outproj_postln",
    )(yacd, yb, x, mod_l, w_all, g, b)


def _ffn_up_kernel(h_ref, wg_ref, wv_ref, cw_ref, cb_ref, o_ref, tail_ref, *, tm, tc):
    @pl.when(pl.program_id(2) == 0)
    def _():
        tail_ref[...] = jnp.zeros(tail_ref.shape, F32)

    h = h_ref[0]
    halo = tail_ref.shape[0]
    for c in range(wg_ref.shape[1] // tc):
        cols = slice(c * tc, (c + 1) * tc)
        gate = _dot(h, wg_ref[:, cols])
        conv = _causal_conv(gate, tail_ref[:, cols], cw_ref.at[:, cols], FFN_K) + cb_ref[:, cols]
        tail_ref[:, cols] = gate[tm - halo:, :]
        o_ref[0, :, cols] = (conv * jax.nn.sigmoid(conv) * _dot(h, wv_ref[:, cols])).astype(BF16)


def _ffn_up(h2, w_up_all, layer, conv_w, conv_b, *, tm, tn, tc):
    bn, seq, d = h2.shape
    d_ff = w_up_all.shape[2] // 2
    nj = d_ff // tn
    assert d_ff == nj * tn and tn % tc == 0
    once = pl.Buffered(1)
    return pl.pallas_call(
        functools.partial(_ffn_up_kernel, tm=tm, tc=tc),
        grid=(nj, bn, seq // tm),
        in_specs=[pl.BlockSpec((1, tm, d), lambda j, b, i: (b, i, 0)),
                  pl.BlockSpec((None, d, tn), lambda j, b, i: (layer, 0, j), pipeline_mode=once),
                  pl.BlockSpec((None, d, tn), lambda j, b, i: (layer, 0, nj + j), pipeline_mode=once),
                  pl.BlockSpec((FFN_K, tn), lambda j, b, i: (0, j)),
                  pl.BlockSpec((1, tn), lambda j, b, i: (0, j))],
        out_specs=pl.BlockSpec((1, tm, tn), lambda j, b, i: (b, i, j)),
        out_shape=jax.ShapeDtypeStruct((bn, seq, d_ff), BF16),
        scratch_shapes=[pltpu.VMEM((_halo_rows(FFN_K), tn), F32)],
        compiler_params=_params(3),
        name="ffn_up_conv",
    )(h2, w_up_all, w_up_all, conv_w, conv_b)


def _ffn_down_kernel(a_ref, x_ref, mod_ref, w_ref, g_ref, b_ref, o_ref, *, alpha, sub_rows):
    mod = mod_ref[0]
    for r in range(x_ref.shape[1] // sub_rows):
        rows = slice(r * sub_rows, (r + 1) * sub_rows)
        y = _dot(a_ref[0, rows, :], w_ref[...])
        o_ref[0, rows, :] = (_norm_rows(alpha * x_ref[0, rows, :] + (1.0 + mod[5:6, :]) * y) * g_ref[...]
                             + b_ref[...])


def _ffn_down(act, x1, mod_l, w_down_all, layer, g, b, *, tm, alpha):
    bn, seq, d = x1.shape
    row = lambda bb, i: (bb, i, 0)
    return pl.pallas_call(
        functools.partial(_ffn_down_kernel, alpha=alpha, sub_rows=min(tm, MXU_COLS)),
        grid=(bn, seq // tm),
        in_specs=[pl.BlockSpec((1, tm, act.shape[2]), row),
                  pl.BlockSpec((1, tm, d), row),
                  pl.BlockSpec((1, 6, d), lambda bb, i: (bb, 0, 0)),
                  _layer_resident(w_down_all.shape, layer), _resident(g.shape), _resident(b.shape)],
        out_specs=pl.BlockSpec((1, tm, d), row),
        out_shape=jax.ShapeDtypeStruct((bn, seq, d), F32),
        compiler_params=_params(2),
        name="ffn_down_postln",
    )(act, x1, mod_l, w_down_all, g, b)


def kernel(x, c, w_mod, b_mod, w_in, gmlp_ln_g, gmlp_ln_b, gmlp_w_s, gmlp_b_s, mla_q_norm, mla_w_uq, mla_kv_norm, mla_w_ukv, sconv_w, conf_w_dw, conf_b_dw, conf_ln_g, conf_ln_b, w_out, post_mix_g, post_mix_b, ffn_w_up, ffn_w_conv, ffn_b_conv, ffn_w_down, post_ffn_g, post_ffn_b):
    bn, seq, d = x.shape
    depth = w_mod.shape[0]
    wg = d // N_GROUPS
    heads = wg // MLA_V
    q_lora, kv_lora = mla_w_uq.shape[1], mla_w_ukv.shape[1]
    alpha = (2.0 * depth) ** 0.25
    t = _tiles(seq)
    row2 = lambda a: a.reshape(1, -1)

    mod = _modulation(c, w_mod, b_mod, t["mod_cols"])
    cs, sn = _rope_tables(seq)
    w_out_bf = w_out.astype(BF16)
    w_up_bf = ffn_w_up.astype(BF16)
    w_down_bf = ffn_w_down.astype(BF16)

    for l in range(depth):
        w_packed = _pack_inproj_weights(w_in[l], wg, q_lora, kv_lora)
        wuq = _pack_uq_weights(mla_w_uq[l], heads)
        b_s_tile = jnp.repeat(gmlp_b_s[l].T, wg // GMLP_HEADS, axis=1)
        yacd, q, k, vt = _inproj(
            x, mod[l], w_packed, cs, sn, row2(gmlp_ln_g[l]), row2(gmlp_ln_b[l]), gmlp_w_s[l],
            b_s_tile, row2(mla_q_norm[l]), wuq, row2(mla_kv_norm[l]), mla_w_ukv[l].astype(BF16),
            sconv_w[l], conf_w_dw[l], row2(conf_b_dw[l]), row2(conf_ln_g[l]), row2(conf_ln_b[l]),
            tm=t["inproj"], wg=wg, heads=heads)
        yb = _attention(q, k, vt, tq=t["attn"], heads=heads)
        x1, h2 = _outproj(yacd, yb, x, mod[l], w_out_bf, l,
                          row2(post_mix_g[l]), row2(post_mix_b[l]), tm=t["outproj"], alpha=alpha)
        act = _ffn_up(h2, w_up_bf, l, ffn_w_conv[l], row2(ffn_b_conv[l]),
                      tm=t["ffn_up"], tn=ffn_w_conv.shape[2] // t["ffn_up_col_blocks"],
                      tc=t["ffn_up_chunk"])
        x = _ffn_down(act, x1, mod[l], w_down_bf, l, row2(post_ffn_g[l]),
                      row2(post_ffn_b[l]), tm=t["ffn_down"], alpha=alpha)
    return x
```

```python
import functools

import jax
import jax.numpy as jnp
from jax import lax
from jax.experimental import pallas as pl
from jax.experimental.pallas import tpu as pltpu

F32 = jnp.float32
BF16 = jnp.bfloat16

CHUNK = 64
N_GROUPS = 4
GMLP_BLOCK = 128
GMLP_HEADS = 4
MLA_NOPE = 128
MLA_ROPE = 64
MLA_V = 128
ROPE_THETA = 10000.0
SCONV_K = 3
CONF_K = 31
FFN_K = 3
LN_EPS = 1e-5
RMS_EPS = 1e-6

LANES = 128
SUBLANES = 8
MXU_COLS = 256
VMEM_LIMIT_BYTES = 56 * 1024 * 1024

ATTN_PAIRS_PER_ITER = 2
NEG_BIG = -1e30
LOG2_E = 1.4426950408889634


def _tiles(seq):
    def pick(pref):
        t = min(pref, seq)
        assert seq % t == 0
        return t
    return dict(inproj=pick(512), attn=pick(1024), outproj=pick(512),
                ffn_up=pick(512), ffn_up_col_blocks=2, ffn_up_chunk=MXU_COLS,
                ffn_down=pick(512), mod_cols=2048)


def _resident(shape):
    nd = len(shape)
    return pl.BlockSpec(shape, lambda *_: (0,) * nd, pipeline_mode=pl.Buffered(1))


def _layer_resident(stacked_shape, layer):
    nd = len(stacked_shape) - 1
    return pl.BlockSpec((None,) + tuple(stacked_shape[1:]), lambda *_: (layer,) + (0,) * nd,
                        pipeline_mode=pl.Buffered(1))


def _params(n_grid):
    return pltpu.CompilerParams(dimension_semantics=("arbitrary",) * n_grid,
                                vmem_limit_bytes=VMEM_LIMIT_BYTES)


def _norm_rows(x):
    mu = jnp.mean(x, axis=-1, keepdims=True)
    xc = x - mu
    var = jnp.mean(xc * xc, axis=-1, keepdims=True)
    return xc * lax.rsqrt(var + LN_EPS)


def _rms_rows(x, gain):
    return x * lax.rsqrt(jnp.mean(x * x, axis=-1, keepdims=True) + RMS_EPS) * gain


def _dot(a, b):
    return jnp.dot(a, b, preferred_element_type=F32)


def _mod_kernel(cb_ref, w_ref, b_ref, o_ref):
    for b in range(cb_ref.shape[0]):
        c = cb_ref[b]
        c_act = c * jax.nn.sigmoid(c)
        for g in range(w_ref.shape[2] // LANES):
            cols = slice(g * LANES, (g + 1) * LANES)
            acc = jnp.sum(w_ref[0, :, cols] * c_act, axis=0, keepdims=True)
            o_ref[0, b:b + 1, cols] = acc + b_ref[0, :, cols]


def _modulation(c, w_mod, b_mod, tn):
    depth, d, n = w_mod.shape
    bn = c.shape[0]
    cb = jnp.broadcast_to(c[:, :, None], (bn, d, LANES))
    out = pl.pallas_call(
        _mod_kernel,
        grid=(depth, n // tn),
        in_specs=[pl.BlockSpec((bn, d, LANES), lambda l, j: (0, 0, 0)),
                  pl.BlockSpec((1, d, tn), lambda l, j: (l, 0, j)),
                  pl.BlockSpec((1, 1, tn), lambda l, j: (l, 0, j))],
        out_specs=pl.BlockSpec((1, bn, tn), lambda l, j: (l, 0, j)),
        out_shape=jax.ShapeDtypeStruct((depth, bn, n), F32),
        compiler_params=_params(2),
        name="modulation",
    )(cb, w_mod, b_mod.reshape(depth, 1, n))
    return out.reshape(depth, bn, 6, d)


def _halo_rows(n_taps):
    return -(-(n_taps - 1) // SUBLANES) * SUBLANES


def _causal_conv(y, tail, w, n_taps):
    halo, tm = tail.shape[0], y.shape[0]
    assert halo % SUBLANES == 0 and halo >= n_taps - 1
    ext = jnp.concatenate([tail, y], axis=0)
    acc = None
    for r in range(min(SUBLANES, n_taps)):
        z = ext if r == 0 else pltpu.roll(ext, r, axis=0)
        for a in range((n_taps - 1 - r) // SUBLANES + 1):
            s = a * SUBLANES + r
            lo = halo - a * SUBLANES
            term = z[lo:lo + tm] * w[n_taps - 1 - s:n_taps - s, :]
            acc = term if acc is None else acc + term
    return acc


def _inproj_kernel(x_ref, mod_ref, w_ref, cs_ref, sn_ref,
                   gln_g_ref, gln_b_ref, ws_ref, bs_ref,
                   qn_g_ref, wuq_ref, kvn_g_ref, wukv_ref,
                   sconv_w_ref, cw_ref, cb_ref, cln_g_ref, cln_b_ref,
                   yacd_ref, q_ref, k_ref, vt_ref,
                   stail_ref, ctail_ref, *, tm, wg, q_lora, kv_lora, heads, scale):
    @pl.when(pl.program_id(1) == 0)
    def _():
        stail_ref[...] = jnp.zeros(stail_ref.shape, F32)
        ctail_ref[...] = jnp.zeros(ctail_ref.shape, F32)

    mod = mod_ref[0]
    h = (_norm_rows(x_ref[0]) * (1.0 + mod[1:2, :]) + mod[0:1, :]).astype(BF16)

    o_uv = 0
    o_q = o_uv + 2 * wg
    o_kv = o_q + q_lora
    o_kr = o_kv + kv_lora
    o_s = o_kr + 2 * LANES
    o_ag = o_s + 3 * wg

    ag = _dot(h, w_ref[:, o_ag:o_ag + 2 * wg])
    glu = ag[:, :wg] * jax.nn.sigmoid(ag[:, wg:])
    y = _causal_conv(glu, ctail_ref[...], cw_ref, CONF_K) + cb_ref[...]
    ctail_ref[...] = glu[tm - ctail_ref.shape[0]:, :]
    y = _norm_rows(y) * cln_g_ref[...] + cln_b_ref[...]
    yacd_ref[0, :, 2 * wg:3 * wg] = (y * jax.nn.sigmoid(y)).astype(BF16)

    s3 = _dot(h, w_ref[:, o_s:o_s + 3 * wg])
    gated = s3[:, wg:2 * wg] * s3[:, 2 * wg:]
    y_c = s3[:, :wg] * _causal_conv(gated, stail_ref[...], sconv_w_ref, SCONV_K)
    stail_ref[...] = gated[tm - stail_ref.shape[0]:, :]
    yacd_ref[0, :, wg:2 * wg] = y_c.astype(BF16)

    z = jax.nn.gelu(_dot(h, w_ref[:, o_uv:o_uv + 2 * wg]), approximate=True)
    u = z[:, :wg]
    v = (_norm_rows(z[:, wg:]) * gln_g_ref[...] + gln_b_ref[...]).astype(BF16)
    nblk = tm // GMLP_BLOCK
    hd = wg // GMLP_HEADS
    ii = lax.broadcasted_iota(jnp.int32, (GMLP_BLOCK, GMLP_BLOCK), 0) // CHUNK
    jj = lax.broadcasted_iota(jnp.int32, (GMLP_BLOCK, GMLP_BLOCK), 1) // CHUNK
    bias = bs_ref[...]
    mixed_cols = []
    for hh in range(GMLP_HEADS):
        w_h = jnp.where(jj <= ii, ws_ref[hh], 0.0).astype(BF16)
        v_h = jnp.concatenate(
            [v[n * GMLP_BLOCK:(n + 1) * GMLP_BLOCK, hh * hd:(hh + 1) * hd] for n in range(nblk)],
            axis=1)
        mixed_cols.append(_dot(w_h, v_h))
    for n in range(nblk):
        mixed = jnp.concatenate([m[:, n * hd:(n + 1) * hd] for m in mixed_cols], axis=1) + bias
        rows = slice(n * GMLP_BLOCK, (n + 1) * GMLP_BLOCK)
        yacd_ref[0, rows, 0:wg] = (u[rows, :] * mixed).astype(BF16)

    cs = cs_ref[...]
    sn = sn_ref[...]
    q_lat = _rms_rows(_dot(h, w_ref[:, o_q:o_q + q_lora]), qn_g_ref[...]).astype(BF16)
    qq = _dot(q_lat, wuq_ref[...])
    kv_lat = _rms_rows(_dot(h, w_ref[:, o_kv:o_kv + kv_lora]), kvn_g_ref[...]).astype(BF16)
    kv = _dot(kv_lat, wukv_ref[...])
    kr2 = _dot(h, w_ref[:, o_kr:o_kr + 2 * LANES])
    k_rope = (kr2[:, :LANES] * cs + kr2[:, LANES:] * sn).astype(BF16)
    hw = 2 * LANES
    for hh in range(heads):
        q_nope = qq[:, hh * hw:hh * hw + LANES]
        q_rope = qq[:, hh * hw + LANES:(hh + 1) * hw]
        q_rot = qq[:, heads * hw + hh * LANES:heads * hw + (hh + 1) * LANES]
        q_ref[0, :, hh * hw:hh * hw + LANES] = (q_nope * scale).astype(BF16)
        q_ref[0, :, hh * hw + LANES:(hh + 1) * hw] = ((q_rope * cs + q_rot * sn) * scale).astype(BF16)
        k_ref[0, :, hh * hw:hh * hw + LANES] = kv[:, hh * hw:hh * hw + LANES].astype(BF16)
        k_ref[0, :, hh * hw + LANES:(hh + 1) * hw] = k_rope
        vt_ref[0, 0, hh * LANES:(hh + 1) * LANES, :] = kv[:, hh * hw + LANES:(hh + 1) * hw].T.astype(BF16)


def _pack_inproj_weights(w_in, wg, q_lora, kv_lora):
    w = w_in.astype(BF16)
    o = 0
    w_uv = w[..., o:o + 2 * wg]; o += 2 * wg
    w_q = w[..., o:o + q_lora]; o += q_lora
    w_kv = w[..., o:o + kv_lora]; o += kv_lora
    w_kr = w[..., o:o + MLA_ROPE]; o += MLA_ROPE
    w_s = w[..., o:o + 3 * wg]; o += 3 * wg
    w_ag = w[..., o:o + 2 * wg]
    half = MLA_ROPE // 2
    pad = jnp.zeros(w.shape[:-1] + (LANES - MLA_ROPE,), BF16)
    w_kr_rot = jnp.concatenate([-w_kr[..., half:], w_kr[..., :half]], axis=-1)
    return jnp.concatenate([w_uv, w_q, w_kv, w_kr, pad, w_kr_rot, pad, w_s, w_ag], axis=-1)


def _pack_uq_weights(w_uq_l, heads):
    r = w_uq_l.shape[0]
    half = MLA_ROPE // 2
    w = w_uq_l.reshape(r, heads, MLA_NOPE + MLA_ROPE)
    pad = jnp.zeros((r, heads, LANES - MLA_ROPE), w.dtype)
    rope = w[:, :, MLA_NOPE:]
    main = jnp.concatenate([w[:, :, :MLA_NOPE], rope, pad], axis=2).reshape(r, heads * 2 * LANES)
    rot = jnp.concatenate([-rope[:, :, half:], rope[:, :, :half], pad], axis=2).reshape(r, heads * LANES)
    return jnp.concatenate([main, rot], axis=1).astype(BF16)


def _rope_tables(seq):
    pos = jnp.arange(seq, dtype=F32)
    inv_freq = ROPE_THETA ** (-jnp.arange(0, MLA_ROPE, 2, dtype=F32) / MLA_ROPE)
    ang = pos[:, None] * inv_freq[None, :]
    zeros = jnp.zeros((seq, LANES - MLA_ROPE), F32)
    cos, sin = jnp.cos(ang), jnp.sin(ang)
    return (jnp.concatenate([cos, cos, zeros], axis=1), jnp.concatenate([sin, sin, zeros], axis=1))


def _inproj(x, mod_l, w_packed_all, layer, cs, sn, gln_g, gln_b, w_s, b_s_tile, qn_g, wuq, kvn_g, wukv,
            sconv_w, conf_w, conf_b, cln_g, cln_b, *, tm, wg, heads):
    bn, seq, d = x.shape
    q_lora, kv_lora = wuq.shape[0], wukv.shape[0]
    s_halo = _halo_rows(SCONV_K)
    c_halo = _halo_rows(CONF_K)
    kern = functools.partial(
        _inproj_kernel, tm=tm, wg=wg, q_lora=q_lora, kv_lora=kv_lora, heads=heads,
        scale=float((MLA_NOPE + MLA_ROPE) ** -0.5 * LOG2_E))
    row = lambda b, i: (b, i, 0)
    smalls = [gln_g, gln_b, w_s, b_s_tile, qn_g, wuq, kvn_g, wukv, sconv_w, conf_w, conf_b,
              cln_g, cln_b]
    return pl.pallas_call(
        kern,
        grid=(bn, seq // tm),
        in_specs=[pl.BlockSpec((1, tm, d), row),
                  pl.BlockSpec((1, 6, d), lambda b, i: (b, 0, 0)),
                  _layer_resident(w_packed_all.shape, layer),
                  pl.BlockSpec((tm, LANES), lambda b, i: (i, 0)),
                  pl.BlockSpec((tm, LANES), lambda b, i: (i, 0))]
                 + [_resident(a.shape) for a in smalls],
        out_specs=[pl.BlockSpec((1, tm, 3 * wg), row),
                   pl.BlockSpec((1, tm, heads * 2 * LANES), row),
                   pl.BlockSpec((1, tm, heads * 2 * LANES), row),
                   pl.BlockSpec((1, 1, heads * LANES, tm), lambda b, i: (b, i, 0, 0))],
        out_shape=[jax.ShapeDtypeStruct((bn, seq, 3 * wg), BF16),
                   jax.ShapeDtypeStruct((bn, seq, heads * 2 * LANES), BF16),
                   jax.ShapeDtypeStruct((bn, seq, heads * 2 * LANES), BF16),
                   jax.ShapeDtypeStruct((bn, seq // tm, heads * LANES, tm), BF16)],
        scratch_shapes=[pltpu.VMEM((s_halo, wg), F32), pltpu.VMEM((c_halo, wg), F32)],
        compiler_params=_params(2),
        name="inproj_mixers",
    )(x, mod_l, w_packed_all, cs, sn, *smalls)


def _attn_kernel(q_ref, k_ref, vt_ref, o_ref, qt_ref, s_ref, mc_ref, m_ref, acc_ref, *, n_sub, tk):
    assert n_sub == 2
    qi = pl.program_id(2)
    first_diag = n_sub * qi
    qt_ref[...] = q_ref[0].astype(F32).T.astype(BF16)
    m_ref[...] = jnp.full(m_ref.shape, NEG_BIG, F32)
    acc_ref[...] = jnp.zeros(acc_ref.shape, F32)
    ones = jnp.ones((acc_ref.shape[1] - MLA_V, tk), BF16)
    all_subs = tuple(range(n_sub))

    def scores(j, slot, subs):
        kb = k_ref[0, pl.ds(pl.multiple_of(j * tk, tk), tk), :]
        for sub in subs:
            st = _dot(kb, qt_ref[:, sub * tk:(sub + 1) * tk])
            s_ref[slot, sub] = st
            mc_ref[slot, sub] = jnp.max(st, axis=0, keepdims=True)

    def update(j, slot, subs, diag_sub=None):
        vtb = jnp.concatenate([vt_ref[0, j], ones], axis=0)
        for sub in subs:
            st = s_ref[slot, sub]
            if sub == diag_sub:
                kc = lax.broadcasted_iota(jnp.int32, (tk, tk), 0) // CHUNK
                qc = lax.broadcasted_iota(jnp.int32, (tk, tk), 1) // CHUNK
                st = jnp.where(kc <= qc, st, NEG_BIG)
                m_cur = jnp.max(st, axis=0, keepdims=True)
            else:
                m_cur = mc_ref[slot, sub]
            m_prev = m_ref[sub]
            m_new = jnp.maximum(m_prev, m_cur)
            p = jnp.exp2(st - m_new).astype(BF16)
            acc_ref[sub] = jnp.exp2(m_prev - m_new) * acc_ref[sub] + _dot(vtb, p)
            m_ref[sub] = m_new

    scores(0, 0, all_subs)

    def tile_pair(j):
        scores(j + 1, 1, all_subs)
        update(j, 0, all_subs)
        scores(j + 2, 0, all_subs)
        update(j + 1, 1, all_subs)

    def body(jj, carry):
        for p in range(ATTN_PAIRS_PER_ITER):
            tile_pair(2 * (ATTN_PAIRS_PER_ITER * jj + p))
        return carry

    lax.fori_loop(0, qi // ATTN_PAIRS_PER_ITER, body, 0)
    for left in range(1, ATTN_PAIRS_PER_ITER):
        @pl.when(qi % ATTN_PAIRS_PER_ITER >= left)
        def _():
            tile_pair(2 * (qi - qi % ATTN_PAIRS_PER_ITER + left - 1))
    scores(first_diag + 1, 1, (1,))
    update(first_diag, 0, all_subs, diag_sub=0)
    update(first_diag + 1, 1, (1,), diag_sub=1)
    for sub in all_subs:
        acc = acc_ref[sub]
        o_ref[0, sub * tk:(sub + 1) * tk, :] = (acc[:MLA_V] / acc[MLA_V:MLA_V + 1]).T.astype(BF16)


def _attention(q, k, vt, *, tq, heads):
    bn, seq, _ = q.shape
    tk = vt.shape[3]
    n_sub = tq // tk
    assert tq == n_sub * tk and n_sub >= 1
    ones_rows = 2 * SUBLANES
    kern = functools.partial(_attn_kernel, n_sub=n_sub, tk=tk)
    return pl.pallas_call(
        kern,
        grid=(bn, heads, seq // tq),
        in_specs=[pl.BlockSpec((1, tq, 2 * LANES), lambda b, h, i: (b, i, h)),
                  pl.BlockSpec((1, seq, 2 * LANES), lambda b, h, i: (b, 0, h)),
                  pl.BlockSpec((1, seq // tk, LANES, tk), lambda b, h, i: (b, 0, h, 0))],
        out_specs=pl.BlockSpec((1, tq, LANES), lambda b, h, i: (b, i, h)),
        out_shape=jax.ShapeDtypeStruct((bn, seq, heads * LANES), BF16),
        scratch_shapes=[pltpu.VMEM((2 * LANES, tq), BF16),
                        pltpu.VMEM((2, n_sub, tk, tk), F32),
                        pltpu.VMEM((2, n_sub, 1, tk), F32),
                        pltpu.VMEM((n_sub, 1, tk), F32),
                        pltpu.VMEM((n_sub, MLA_V + ones_rows, tk), F32)],
        compiler_params=_params(3),
        name="mla_attention",
    )(q, k, vt)


def _outproj_kernel(yacd_ref, yb_ref, x_ref, mod_ref, w_ref, g_ref, b_ref,
                    x1_ref, h2_ref, *, alpha, sub_rows):
    mod = mod_ref[0]
    wg = yb_ref.shape[2]
    for r in range(x_ref.shape[1] // sub_rows):
        rows = slice(r * sub_rows, (r + 1) * sub_rows)
        y = (_dot(yacd_ref[0, rows, :wg], w_ref[:wg, :]) + _dot(yb_ref[0, rows, :], w_ref[wg:2 * wg, :])
             + _dot(yacd_ref[0, rows, wg:], w_ref[2 * wg:, :]))
        x1 = _norm_rows(alpha * x_ref[0, rows, :] + (1.0 + mod[2:3, :]) * y) * g_ref[...] + b_ref[...]
        x1_ref[0, rows, :] = x1
        h2_ref[0, rows, :] = (_norm_rows(x1) * (1.0 + mod[4:5, :]) + mod[3:4, :]).astype(BF16)


def _outproj(yacd, yb, x, mod_l, w_all, layer, g, b, *, tm, alpha):
    bn, seq, d = x.shape
    row = lambda bb, i: (bb, i, 0)
    return pl.pallas_call(
        functools.partial(_outproj_kernel, alpha=alpha, sub_rows=min(tm, MXU_COLS)),
        grid=(bn, seq // tm),
        in_specs=[pl.BlockSpec((1, tm, yacd.shape[2]), row),
                  pl.BlockSpec((1, tm, yb.shape[2]), row),
                  pl.BlockSpec((1, tm, d), row),
                  pl.BlockSpec((1, 6, d), lambda bb, i: (bb, 0, 0)),
                  _layer_resident(w_all.shape, layer),
                  _resident(g.shape), _resident(b.shape)],
        out_specs=[pl.BlockSpec((1, tm, d), row), pl.BlockSpec((1, tm, d), row)],
        out_shape=[jax.ShapeDtypeStruct((bn, seq, d), F32),
                   jax.ShapeDtypeStruct((bn, seq, d), BF16)],
        compiler_params=_params(2),
        name="outproj_postln",
    )(yacd, yb, x, mod_l, w_all, g, b)


def _ffn_up_kernel(h_ref, wg_ref, wv_ref, cw_ref, cb_ref, o_ref, tail_ref, *, tm, tc):
    @pl.when(pl.program_id(2) == 0)
    def _():
        tail_ref[...] = jnp.zeros(tail_ref.shape, F32)

    h = h_ref[0]
    halo = tail_ref.shape[0]
    for c in range(wg_ref.shape[1] // tc):
        cols = slice(c * tc, (c + 1) * tc)
        gate = _dot(h, wg_ref[:, cols])
        conv = _causal_conv(gate, tail_ref[:, cols], cw_ref.at[:, cols], FFN_K) + cb_ref[:, cols]
        tail_ref[:, cols] = gate[tm - halo:, :]
        o_ref[0, :, cols] = (conv * jax.nn.sigmoid(conv) * _dot(h, wv_ref[:, cols])).astype(BF16)


def _ffn_up(h2, w_up_all, layer, conv_w, conv_b, *, tm, tn, tc):
    bn, seq, d = h2.shape
    d_ff = w_up_all.shape[2] // 2
    nj = d_ff // tn
    assert d_ff == nj * tn and tn % tc == 0
    once = pl.Buffered(1)
    return pl.pallas_call(
        functools.partial(_ffn_up_kernel, tm=tm, tc=tc),
        grid=(nj, bn, seq // tm),
        in_specs=[pl.BlockSpec((1, tm, d), lambda j, b, i: (b, i, 0)),
                  pl.BlockSpec((None, d, tn), lambda j, b, i: (layer, 0, j), pipeline_mode=once),
                  pl.BlockSpec((None, d, tn), lambda j, b, i: (layer, 0, nj + j), pipeline_mode=once),
                  pl.BlockSpec((FFN_K, tn), lambda j, b, i: (0, j)),
                  pl.BlockSpec((1, tn), lambda j, b, i: (0, j))],
        out_specs=pl.BlockSpec((1, tm, tn), lambda j, b, i: (b, i, j)),
        out_shape=jax.ShapeDtypeStruct((bn, seq, d_ff), BF16),
        scratch_shapes=[pltpu.VMEM((_halo_rows(FFN_K), tn), F32)],
        compiler_params=_params(3),
        name="ffn_up_conv",
    )(h2, w_up_all, w_up_all, conv_w, conv_b)


def _ffn_down_kernel(a_ref, x_ref, mod_ref, w_ref, g_ref, b_ref, o_ref, *, alpha, sub_rows):
    mod = mod_ref[0]
    for r in range(x_ref.shape[1] // sub_rows):
        rows = slice(r * sub_rows, (r + 1) * sub_rows)
        y = _dot(a_ref[0, rows, :], w_ref[...])
        o_ref[0, rows, :] = (_norm_rows(alpha * x_ref[0, rows, :] + (1.0 + mod[5:6, :]) * y) * g_ref[...]
                             + b_ref[...])


def _ffn_down(act, x1, mod_l, w_down_all, layer, g, b, *, tm, alpha):
    bn, seq, d = x1.shape
    row = lambda bb, i: (bb, i, 0)
    return pl.pallas_call(
        functools.partial(_ffn_down_kernel, alpha=alpha, sub_rows=min(tm, MXU_COLS)),
        grid=(bn, seq // tm),
        in_specs=[pl.BlockSpec((1, tm, act.shape[2]), row),
                  pl.BlockSpec((1, tm, d), row),
                  pl.BlockSpec((1, 6, d), lambda bb, i: (bb, 0, 0)),
                  _layer_resident(w_down_all.shape, layer), _resident(g.shape), _resident(b.shape)],
        out_specs=pl.BlockSpec((1, tm, d), row),
        out_shape=jax.ShapeDtypeStruct((bn, seq, d), F32),
        compiler_params=_params(2),
        name="ffn_down_postln",
    )(act, x1, mod_l, w_down_all, g, b)


def kernel(x, c, w_mod, b_mod, w_in, gmlp_ln_g, gmlp_ln_b, gmlp_w_s, gmlp_b_s, mla_q_norm, mla_w_uq, mla_kv_norm, mla_w_ukv, sconv_w, conf_w_dw, conf_b_dw, conf_ln_g, conf_ln_b, w_out, post_mix_g, post_mix_b, ffn_w_up, ffn_w_conv, ffn_b_conv, ffn_w_down, post_ffn_g, post_ffn_b):
    bn, seq, d = x.shape
    depth = w_mod.shape[0]
    wg = d // N_GROUPS
    heads = wg // MLA_V
    q_lora, kv_lora = mla_w_uq.shape[1], mla_w_ukv.shape[1]
    alpha = (2.0 * depth) ** 0.25
    t = _tiles(seq)
    row2 = lambda a: a.reshape(1, -1)

    mod = _modulation(c, w_mod, b_mod, t["mod_cols"])
    cs, sn = _rope_tables(seq)
    w_in_packed = _pack_inproj_weights(w_in, wg, q_lora, kv_lora)
    w_out_bf = w_out.astype(BF16)
    w_up_bf = ffn_w_up.astype(BF16)
    w_down_bf = ffn_w_down.astype(BF16)

    for l in range(depth):
        wuq = _pack_uq_weights(mla_w_uq[l], heads)
        b_s_tile = jnp.repeat(gmlp_b_s[l].T, wg // GMLP_HEADS, axis=1)
        yacd, q, k, vt = _inproj(
            x, mod[l], w_in_packed, l, cs, sn, row2(gmlp_ln_g[l]), row2(gmlp_ln_b[l]), gmlp_w_s[l],
            b_s_tile, row2(mla_q_norm[l]), wuq, row2(mla_kv_norm[l]), mla_w_ukv[l].astype(BF16),
            sconv_w[l], conf_w_dw[l], row2(conf_b_dw[l]), row2(conf_ln_g[l]), row2(conf_ln_b[l]),
            tm=t["inproj"], wg=wg, heads=heads)
        yb = _attention(q, k, vt, tq=t["attn"], heads=heads)
        x1, h2 = _outproj(yacd, yb, x, mod[l], w_out_bf, l,
                          row2(post_mix_g[l]), row2(post_mix_b[l]), tm=t["outproj"], alpha=alpha)
        act = _ffn_up(h2, w_up_bf, l, ffn_w_conv[l], row2(ffn_b_conv[l]),
                      tm=t["ffn_up"], tn=ffn_w_conv.shape[2] // t["ffn_up_col_blocks"],
                      tc=t["ffn_up_chunk"])
        x = _ffn_down(act, x1, mod[l], w_down_bf, l, row2(post_ffn_g[l]),
                      row2(post_ffn_b[l]), tm=t["ffn_down"], alpha=alpha)
    return x
```
